```python
import jax, jax.numpy as jnp
from jax import lax
import numpy as np

D_MODEL = 2048
BATCH = 2
SEQ = 8192
DEPTH = 2

A_HEADS = 4
A_HEAD_DIM = 256
A_WIDTH = A_HEADS * A_HEAD_DIM
A_CHUNK = 64
B_GROUPS = 4
B_GROUP_DIM = 256
B_WIDTH = B_GROUPS * B_GROUP_DIM
B_CHUNK = 128
MIX_WIDTH = A_WIDTH + B_WIDTH
EVEN_SPLITS = (A_WIDTH, 2 * A_WIDTH, 3 * A_WIDTH, 4 * A_WIDTH,
               4 * A_WIDTH + A_HEADS, 4 * A_WIDTH + 2 * A_HEADS,
               4 * A_WIDTH + 2 * A_HEADS + B_WIDTH)
EVEN_IN_COLS = 4 * A_WIDTH + 2 * A_HEADS + 2 * B_WIDTH

C_HEADS = 16
C_KV_HEADS = 4
C_GROUP = C_HEADS // C_KV_HEADS
C_HEAD_DIM = 128
IDX_HEADS = 16
IDX_DIM = 64
IDX_ROPE_DIM = 32
TOPK_MAX = 256
Q_BLOCK = 128
ROPE_THETA = 10000.0
ODD_SPLITS = (C_HEADS * C_HEAD_DIM,
              C_HEADS * C_HEAD_DIM + C_KV_HEADS * C_HEAD_DIM,
              C_HEADS * C_HEAD_DIM + 2 * C_KV_HEADS * C_HEAD_DIM,
              C_HEADS * C_HEAD_DIM + 2 * C_KV_HEADS * C_HEAD_DIM + IDX_HEADS * IDX_DIM,
              C_HEADS * C_HEAD_DIM + 2 * C_KV_HEADS * C_HEAD_DIM + IDX_HEADS * IDX_DIM + IDX_DIM)
ODD_IN_COLS = C_HEADS * C_HEAD_DIM + 2 * C_KV_HEADS * C_HEAD_DIM + IDX_HEADS * IDX_DIM + IDX_DIM + IDX_HEADS

FFN_DIM = 5632
CONV_WIDTH = 3
NORM_EPS = 1e-6
N_EVEN = (DEPTH + 1) // 2
N_ODD = DEPTH // 2

kernel_name = 'hybrid_mlstm_sgu_dsa_convffn'


def rmsnorm(x, g):
    xf = x.astype(jnp.float32)
    y = xf * lax.rsqrt(jnp.mean(xf * xf, axis=-1, keepdims=True) + NORM_EPS)
    return (y * g.astype(jnp.float32)).astype(x.dtype)


def layernorm(x, g, b):
    xf = x.astype(jnp.float32)
    mu = jnp.mean(xf, axis=-1, keepdims=True)
    var = jnp.mean(jnp.square(xf - mu), axis=-1, keepdims=True)
    y = (xf - mu) * lax.rsqrt(var + NORM_EPS)
    return (y * g.astype(jnp.float32) + b.astype(jnp.float32)).astype(x.dtype)


def rope(x, pos):
    d = x.shape[-1]
    inv = jnp.power(jnp.float32(ROPE_THETA), -jnp.arange(0, d, 2, dtype=jnp.float32) / d)
    ang = pos.astype(jnp.float32)[:, None] * inv[None, :]
    cos = jnp.cos(ang)[:, None, :]
    sin = jnp.sin(ang)[:, None, :]
    xf = x.astype(jnp.float32)
    x1, x2 = xf[..., : d // 2], xf[..., d // 2:]
    return jnp.concatenate([x1 * cos - x2 * sin, x2 * cos + x1 * sin], axis=-1).astype(x.dtype)


def partial_rope(x, pos):
    return jnp.concatenate([rope(x[..., :IDX_ROPE_DIM], pos), x[..., IDX_ROPE_DIM:]], axis=-1)


def mlstm_chunkwise(q, k, v, i_pre, f_pre):
    Bn, H, S, d = q.shape
    L = A_CHUNK
    nc = S // L
    q = q * (d ** -0.5)
    lf = jax.nn.log_sigmoid(f_pre)

    def to_chunks(a):
        return jnp.moveaxis(a.reshape(Bn, H, nc, L, *a.shape[3:]), 2, 0)

    xs = (to_chunks(q), to_chunks(k), to_chunks(v), to_chunks(i_pre), to_chunks(lf))
    causal = jnp.tril(jnp.ones((L, L), dtype=bool))

    def step(carry, inp):
        C, n, m = carry
        qc, kc, vc, ic, fc = inp
        b = jnp.cumsum(fc, axis=-1)
        dmat = jnp.where(causal, b[..., :, None] - b[..., None, :] + ic[..., None, :], -jnp.inf)
        m_inter = b + m[..., None]
        m_t = jnp.maximum(m_inter, jnp.max(dmat, axis=-1))
        s = jnp.einsum('bhtd,bhsd->bhts', qc, kc) * jnp.exp(dmat - m_t[..., None])
        w_inter = jnp.exp(m_inter - m_t)
        num = jnp.einsum('bhts,bhsd->bhtd', s, vc) + w_inter[..., None] * jnp.einsum('bhtk,bhkv->bhtv', qc, C)
        den = jnp.sum(s, axis=-1) + w_inter * jnp.einsum('bhtk,bhk->bht', qc, n)
        h = num / jnp.maximum(jnp.abs(den), jnp.exp(-m_t))[..., None]
        b_last = b[..., -1]
        g = b_last[..., None] - b + ic
        m_new = jnp.maximum(b_last + m, jnp.max(g, axis=-1))
        decay = jnp.exp(b_last + m - m_new)
        wk = jnp.exp(g - m_new[..., None])[..., None] * kc
        C_new = decay[..., None, None] * C + jnp.einsum('bhsk,bhsv->bhkv', wk, vc)
        n_new = decay[..., None] * n + jnp.sum(wk, axis=2)
        return (C_new, n_new, m_new), h

    init = (jnp.zeros((Bn, H, d, d), jnp.float32), jnp.zeros((Bn, H, d), jnp.float32),
            jnp.zeros((Bn, H), jnp.float32))
    _, hs = lax.scan(step, init, xs)
    return jnp.moveaxis(hs, 0, 2).reshape(Bn, H, S, d)


def even_mixer(xn, w_in, i_bias, f_bias, a_norm, b_norm, spatial, spatial_bias, w_out):
    Bn, S, _ = xn.shape
    proj = xn @ w_in
    q, k, v, o, i_pre, f_pre, u, z = jnp.split(proj, EVEN_SPLITS, axis=-1)

    def heads(t):
        return t.reshape(Bn, S, A_HEADS, A_HEAD_DIM).transpose(0, 2, 1, 3).astype(jnp.float32)
    i_pre = (i_pre + i_bias).astype(jnp.float32).transpose(0, 2, 1)
    f_pre = (f_pre + f_bias).astype(jnp.float32).transpose(0, 2, 1)
    h = mlstm_chunkwise(heads(q), heads(k), heads(v), i_pre, f_pre)
    h = h.transpose(0, 2, 1, 3).astype(xn.dtype)
    h = rmsnorm(h, a_norm) * jax.nn.sigmoid(o).reshape(Bn, S, A_HEADS, A_HEAD_DIM)
    h_a = h.reshape(Bn, S, A_WIDTH)

    nch = S // B_CHUNK
    u = jax.nn.gelu(u)
    z = rmsnorm(jax.nn.gelu(z).reshape(Bn, nch, B_CHUNK, B_GROUPS, B_GROUP_DIM), b_norm)
    w_s = jnp.where(jnp.tril(jnp.ones((B_CHUNK, B_CHUNK), dtype=bool)), spatial, 0.0)
    zmix = jnp.einsum('gts,bcsgd->bctgd', w_s, z) + spatial_bias.T[:, :, None]
    h_b = u * zmix.reshape(Bn, S, B_WIDTH)

    return jnp.concatenate([h_a, h_b], axis=-1) @ w_out


def odd_mixer(xn, pos, w_in, q_norm, k_norm, idx_ln_g, idx_ln_b, w_out):
    Bn, S, _ = xn.shape
    proj = xn @ w_in
    q, k, v, iq, ik, iw = jnp.split(proj, ODD_SPLITS, axis=-1)
    q = rope(rmsnorm(q.reshape(Bn, S, C_HEADS, C_HEAD_DIM), q_norm), pos)
    k = rope(rmsnorm(k.reshape(Bn, S, C_KV_HEADS, C_HEAD_DIM), k_norm), pos)
    v = v.reshape(Bn, S, C_KV_HEADS, C_HEAD_DIM)
    iq = partial_rope(iq.reshape(Bn, S, IDX_HEADS, IDX_DIM), pos)
    ik = partial_rope(layernorm(ik, idx_ln_g, idx_ln_b)[:, :, None, :], pos)[:, :, 0, :]
    iw = iw * (IDX_HEADS ** -0.5 * IDX_DIM ** -0.5)
    topk = min(TOPK_MAX, S // 4)
    nqb = S // Q_BLOCK
    ik32 = ik.astype(jnp.float32)

    qb_all = jnp.moveaxis(q.reshape(Bn, nqb, Q_BLOCK, C_KV_HEADS, C_GROUP, C_HEAD_DIM), 1, 0)
    iqb_all = jnp.moveaxis(iq.reshape(Bn, nqb, Q_BLOCK, IDX_HEADS, IDX_DIM), 1, 0)
    iwb_all = jnp.moveaxis(iw.reshape(Bn, nqb, Q_BLOCK, IDX_HEADS), 1, 0)
    tb_all = pos.reshape(nqb, Q_BLOCK)

    def block(inp):
        qb, iqb, iwb, tb = inp
        sc = jax.nn.relu(jnp.einsum('bqhd,bsd->bqhs', iqb.astype(jnp.float32), ik32))
        sc = jnp.einsum('bqhs,bqh->bqs', sc, iwb.astype(jnp.float32))
        causal = pos[None, :] <= tb[:, None]
        sc = jnp.where(causal[None], sc, -jnp.inf)
        _, idx = lax.top_k(sc, topk)
        valid = idx <= tb[None, :, None]
        ksel = jax.vmap(lambda kk, ii: kk[ii])(k, idx)
        vsel = jax.vmap(lambda vv, ii: vv[ii])(v, idx)
        logits = jnp.einsum('bqhgd,bqkhd->bqhgk', qb.astype(jnp.float32),
                            ksel.astype(jnp.float32)) * (C_HEAD_DIM ** -0.5)
        logits = jnp.where(valid[:, :, None, None, :], logits, -jnp.inf)
        p = jax.nn.softmax(logits, axis=-1)
        ob = jnp.einsum('bqhgk,bqkhd->bqhgd', p, vsel.astype(jnp.float32))
        return ob.astype(qb.dtype)

    out = lax.map(block, (qb_all, iqb_all, iwb_all, tb_all))
    out = jnp.moveaxis(out, 0, 1).reshape(Bn, S, C_HEADS * C_HEAD_DIM)
    return out @ w_out


def conv_ffn(xn, w_up, conv_w, conv_b, w_down):
    h = xn @ w_up
    ch = h.shape[-1]
    h = lax.conv_general_dilated(h, conv_w[:, None, :], window_strides=(1,),
                                 padding=[(CONV_WIDTH - 1, 0)],
                                 dimension_numbers=('NWC', 'WIO', 'NWC'),
                                 feature_group_count=ch) + conv_b
    a, b = jnp.split(h, 2, axis=-1)
    return (jax.nn.silu(a) * b) @ w_down


def setup_inputs(seed: int = 0) -> dict:
    key = jax.random.key(seed)
    ks = jax.random.split(key, 24)
    nrm = jax.random.normal
    f32 = jnp.float32
    D = D_MODEL
    return {
        'x': nrm(ks[0], (BATCH, SEQ, D), f32),
        'mix_norm': 1.0 + 0.02 * nrm(ks[1], (DEPTH, D), f32),
        'even_w_in': nrm(ks[2], (N_EVEN, D, EVEN_IN_COLS), f32) * D ** -0.5,
        'even_i_bias': 0.1 * nrm(ks[3], (N_EVEN, A_HEADS), f32),
        'even_f_bias': jnp.linspace(3.0, 6.0, A_HEADS, dtype=f32)[None, :] + 0.1 * nrm(ks[4], (N_EVEN, A_HEADS), f32),
        'even_a_norm': 1.0 + 0.02 * nrm(ks[5], (N_EVEN, A_HEADS, A_HEAD_DIM), f32),
        'even_b_norm': 1.0 + 0.02 * nrm(ks[6], (N_EVEN, B_GROUPS, B_GROUP_DIM), f32),
        'even_spatial': nrm(ks[7], (N_EVEN, B_GROUPS, B_CHUNK, B_CHUNK), f32) * B_CHUNK ** -0.5,
        'even_spatial_bias': 1.0 + 0.02 * nrm(ks[8], (N_EVEN, B_GROUPS, B_CHUNK), f32),
        'even_w_out': nrm(ks[9], (N_EVEN, MIX_WIDTH, D), f32) * MIX_WIDTH ** -0.5,
        'odd_w_in': nrm(ks[10], (N_ODD, D, ODD_IN_COLS), f32) * D ** -0.5,
        'odd_q_norm': 1.0 + 0.02 * nrm(ks[11], (N_ODD, C_HEAD_DIM), f32),
        'odd_k_norm': 1.0 + 0.02 * nrm(ks[12], (N_ODD, C_HEAD_DIM), f32),
        'odd_idx_ln_g': 1.0 + 0.02 * nrm(ks[13], (N_ODD, IDX_DIM), f32),
        'odd_idx_ln_b': 0.02 * nrm(ks[14], (N_ODD, IDX_DIM), f32),
        'odd_w_out': nrm(ks[15], (N_ODD, C_HEADS * C_HEAD_DIM, D), f32) * (C_HEADS * C_HEAD_DIM) ** -0.5,
        'ffn_norm': 1.0 + 0.02 * nrm(ks[16], (DEPTH, D), f32),
        'ffn_w_up': nrm(ks[17], (DEPTH, D, 2 * FFN_DIM), f32) * D ** -0.5,
        'ffn_conv_w': nrm(ks[18], (DEPTH, CONV_WIDTH, 2 * FFN_DIM), f32) * CONV_WIDTH ** -0.5,
        'ffn_conv_b': 0.02 * nrm(ks[19], (DEPTH, 2 * FFN_DIM), f32),
        'ffn_w_down': nrm(ks[20], (DEPTH, FFN_DIM, D), f32) * FFN_DIM ** -0.5,
    }


def reference(x, mix_norm, even_w_in, even_i_bias, even_f_bias, even_a_norm, even_b_norm,
              even_spatial, even_spatial_bias, even_w_out, odd_w_in, odd_q_norm, odd_k_norm,
              odd_idx_ln_g, odd_idx_ln_b, odd_w_out, ffn_norm, ffn_w_up, ffn_conv_w,
              ffn_conv_b, ffn_w_down):
    pos = jnp.arange(x.shape[1], dtype=jnp.int32)
    h = x
    for layer in range(DEPTH):
        j = layer // 2
        xn = rmsnorm(h, mix_norm[layer])
        if layer % 2 == 0:
            h = h + even_mixer(xn, even_w_in[j], even_i_bias[j], even_f_bias[j], even_a_norm[j],
                               even_b_norm[j], even_spatial[j], even_spatial_bias[j], even_w_out[j])
        else:
            h = h + odd_mixer(xn, pos, odd_w_in[j], odd_q_norm[j], odd_k_norm[j],
                              odd_idx_ln_g[j], odd_idx_ln_b[j], odd_w_out[j])
        h = h + conv_ffn(rmsnorm(h, ffn_norm[layer]), ffn_w_up[layer], ffn_conv_w[layer],
                         ffn_conv_b[layer], ffn_w_down[layer])
    return h
```

```python
import functools

import jax
import jax.numpy as jnp
from jax import lax
from jax.experimental import pallas as pl
from jax.experimental.pallas import tpu as pltpu

A_HEADS = 4
A_HEAD_DIM = 256
B_GROUPS = 4
B_GROUP_DIM = 256
B_CHUNK = 128
C_HEADS = 16
C_KV_HEADS = 4
C_HEAD_DIM = 128
IDX_HEADS = 16
IDX_DIM = 64
IDX_ROPE_DIM = 32
TOPK_MAX = 256
ROPE_THETA = 10000.0
CONV_WIDTH = 3
NORM_EPS = 1e-6

A_WIDTH = A_HEADS * A_HEAD_DIM
B_WIDTH = B_GROUPS * B_GROUP_DIM

LANES = 128
HALO = 16
MLSTM_CHUNK = 256
MASK_NEG = -1e30
INT_MIN = -(2 ** 31)

F32 = jnp.float32
BF16 = jnp.bfloat16


def _params(dims, vmem_mb):
    return pltpu.CompilerParams(dimension_semantics=dims, vmem_limit_bytes=vmem_mb * 1024 * 1024)


def _rms(x, g):
    return x * lax.rsqrt(jnp.mean(x * x, axis=-1, keepdims=True) + NORM_EPS) * g


def _log_sigmoid(x):
    return jnp.minimum(x, 0.0) - jnp.log1p(jnp.exp(-jnp.abs(x)))


def _dot(a, b):
    return jnp.dot(a, b, preferred_element_type=F32)


def _dot_nt(a, b):
    return lax.dot_general(a, b, (((1,), (1,)), ((), ())), preferred_element_type=F32)


def _dot_tn(a, b):
    return lax.dot_general(a, b, (((0,), (0,)), ((), ())), preferred_element_type=F32)


def _tile(n, pref):
    t = min(n, pref)
    assert n % t == 0, (n, pref)
    return t


def _norm_matmul_kernel(x_ref, g_ref, w_ref, o_ref, xn_ref):
    @pl.when(pl.program_id(1) == 0)
    def _():
        xn_ref[...] = _rms(x_ref[...], g_ref[...]).astype(BF16)

    o_ref[...] = _dot(xn_ref[...], w_ref[...]).astype(o_ref.dtype)


def _norm_matmul(x, g, w, out_dtype=F32, tm=512, tn=512):
    m, d = x.shape
    n = w.shape[1]
    tm, tn = _tile(m, tm), _tile(n, tn)
    return pl.pallas_call(
        _norm_matmul_kernel,
        grid=(m // tm, n // tn),
        in_specs=[pl.BlockSpec((tm, d), lambda i, j: (i, 0)),
                  pl.BlockSpec((1, d), lambda i, j: (0, 0)),
                  pl.BlockSpec((d, tn), lambda i, j: (0, j))],
        out_specs=pl.BlockSpec((tm, tn), lambda i, j: (i, j)),
        out_shape=jax.ShapeDtypeStruct((m, n), out_dtype),
        scratch_shapes=[pltpu.VMEM((tm, d), BF16)],
        compiler_params=_params(("parallel", "arbitrary"), 48),
        name="norm_matmul",
    )(x, g.reshape(1, d), w)


def _matmul_res_kernel(*refs, n_lhs):
    lhs, ws = refs[:n_lhs], refs[n_lhs:2 * n_lhs]
    res_ref, o_ref = refs[2 * n_lhs], refs[2 * n_lhs + 1]
    acc = res_ref[...]
    for a_ref, w_ref in zip(lhs, ws):
        acc = acc + _dot(a_ref[...], w_ref[...])
    o_ref[...] = acc


def _matmul_res(lhs_list, w_list, res, tm=512, tn=512):
    m, n = res.shape
    tm, tn = _tile(m, tm), _tile(n, tn)
    in_specs = ([pl.BlockSpec((tm, a.shape[1]), lambda i, j: (i, 0)) for a in lhs_list]
                + [pl.BlockSpec((w.shape[0], tn), lambda i, j: (0, j)) for w in w_list]
                + [pl.BlockSpec((tm, tn), lambda i, j: (i, j))])
    return pl.pallas_call(
        functools.partial(_matmul_res_kernel, n_lhs=len(lhs_list)),
        grid=(m // tm, n // tn),
        in_specs=in_specs,
        out_specs=pl.BlockSpec((tm, tn), lambda i, j: (i, j)),
        out_shape=jax.ShapeDtypeStruct((m, n), F32),
        compiler_params=_params(("parallel", "parallel"), 48),
        name="matmul_res",
    )(*lhs_list, *w_list, res)


def _ffn_up_kernel(x_ref, halo_ref, g_ref, wa_ref, wb_ref, cwa_ref, cwb_ref, cba_ref, cbb_ref, o_ref,
                   xn_ref, sa_ref, sb_ref, *, tm, seq):
    i = pl.program_id(0)

    @pl.when(pl.program_id(1) == 0)
    def _():
        g = g_ref[...]
        starts_sequence = (i * tm) % seq == 0
        halo = jnp.where(starts_sequence, 0.0, _rms(halo_ref[...], g))
        xn_ref[0:HALO, :] = halo.astype(BF16)
        xn_ref[HALO:, :] = _rms(x_ref[...], g).astype(BF16)

    xn = xn_ref[...]
    sa_ref[...] = _dot(xn, wa_ref[...])
    sb_ref[...] = _dot(xn, wb_ref[...])

    def conv(s_ref, cw_ref, cb_ref):
        cw = cw_ref[...]
        out = cb_ref[...]
        for tap in range(CONV_WIDTH):
            lo = HALO - (CONV_WIDTH - 1) + tap
            out = out + cw[tap:tap + 1, :] * s_ref[lo:lo + tm, :]
        return out

    a = conv(sa_ref, cwa_ref, cba_ref)
    b = conv(sb_ref, cwb_ref, cbb_ref)
    o_ref[...] = (a * jax.nn.sigmoid(a) * b).astype(o_ref.dtype)


def _ffn_up(x, g, w_up, conv_w, conv_b, seq, tm=512, tn=512):
    m, d = x.shape
    f = w_up.shape[1] // 2
    tm, tn = _tile(seq, tm), _tile(f, tn)
    nj = f // tn
    conv_b = conv_b.reshape(1, 2 * f)
    return pl.pallas_call(
        functools.partial(_ffn_up_kernel, tm=tm, seq=seq),
        grid=(m // tm, nj),
        in_specs=[pl.BlockSpec((tm, d), lambda i, j: (i, 0)),
                  pl.BlockSpec((HALO, d), lambda i, j: (jnp.maximum(i * (tm // HALO) - 1, 0), 0)),
                  pl.BlockSpec((1, d), lambda i, j: (0, 0)),
                  pl.BlockSpec((d, tn), lambda i, j: (0, j)),
                  pl.BlockSpec((d, tn), lambda i, j: (0, j + nj)),
                  pl.BlockSpec((CONV_WIDTH, tn), lambda i, j: (0, j)),
                  pl.BlockSpec((CONV_WIDTH, tn), lambda i, j: (0, j + nj)),
                  pl.BlockSpec((1, tn), lambda i, j: (0, j)),
                  pl.BlockSpec((1, tn), lambda i, j: (0, j + nj))],
        out_specs=pl.BlockSpec((tm, tn), lambda i, j: (i, j)),
        out_shape=jax.ShapeDtypeStruct((m, f), BF16),
        scratch_shapes=[pltpu.VMEM((tm + HALO, d), BF16),
                        pltpu.VMEM((tm + HALO, tn), F32),
                        pltpu.VMEM((tm + HALO, tn), F32)],
        compiler_params=_params(("parallel", "arbitrary"), 48),
        name="ffn_up",
    )(x, x, g.reshape(1, d), w_up, w_up, conv_w, conv_w, conv_b, conv_b)


def _conv_ffn(h, g, w_up, conv_w, conv_b, w_down, seq):
    gated = _ffn_up(h, g, w_up.astype(BF16), conv_w, conv_b, seq)
    return _matmul_res([gated], [w_down.astype(BF16)], h)


def _mlstm_kernel(q_ref, k_ref, v_ref, o_ref, grow_ref, gcol_ref, bcol_ref, brow_ref, an_ref, out_ref,
                  c_ref, n_ref, m_ref, *, L):
    H, dh = A_HEADS, A_HEAD_DIM

    @pl.when(pl.program_id(1) == 0)
    def _():
        c_ref[...] = jnp.zeros_like(c_ref)
        n_ref[...] = jnp.zeros_like(n_ref)
        m_ref[...] = jnp.zeros_like(m_ref)

    row = lax.broadcasted_iota(jnp.int32, (L, L), 0)
    col = lax.broadcasted_iota(jnp.int32, (L, L), 1)
    lower = col <= row
    upper = row <= col
    grow = grow_ref[...] + bcol_ref[...]
    gcol = gcol_ref[...] + brow_ref[...]

    for h in range(H):
        hs = slice(h * dh, (h + 1) * dh)
        i_row, lf_row = grow[h:h + 1, :], _log_sigmoid(grow[H + h:H + h + 1, :])
        i_col, lf_col = gcol[:, h:h + 1], _log_sigmoid(gcol[:, H + h:H + h + 1])
        b_col = jnp.sum(jnp.where(lower, lf_row, 0.0), axis=1, keepdims=True)
        b_row = jnp.sum(jnp.where(upper, lf_col, 0.0), axis=0, keepdims=True)
        a_row = i_row - b_row
        a_col = i_col - b_col
        m_prev = m_ref[h][:, 0:1]
        m_col = jnp.maximum(m_prev, jnp.max(jnp.where(lower, a_row, -jnp.inf), axis=1, keepdims=True))
        e = jnp.where(lower, jnp.exp(a_row - m_col), 0.0)

        q = q_ref[:, hs] * (dh ** -0.5)
        kf = k_ref[:, hs]
        qb, kb, vb = q.astype(BF16), kf.astype(BF16), v_ref[:, hs].astype(BF16)
        s = _dot_nt(qb, kb) * e
        w_inter = jnp.exp(m_prev - m_col)
        num = _dot(s.astype(BF16), vb) + w_inter * _dot(qb, c_ref[h].astype(BF16))
        den = jnp.sum(s, axis=1, keepdims=True) + w_inter * jnp.sum(q * n_ref[h], axis=1, keepdims=True)
        hh = num / jnp.maximum(jnp.abs(den), jnp.exp(-(b_col + m_col)))

        m_last = m_col[L - 1:L, :]
        decay = jnp.exp(m_prev - m_last)
        wk = jnp.exp(a_col - m_last) * kf
        c_ref[h] = decay * c_ref[h] + _dot_tn(wk.astype(BF16), vb)
        n_ref[h] = decay * n_ref[h] + jnp.sum(wk, axis=0, keepdims=True)
        m_ref[h] = jnp.broadcast_to(b_col[L - 1:L, :] + m_last, (1, LANES))

        y = _rms(hh, an_ref[h:h + 1, :]) * jax.nn.sigmoid(o_ref[:, hs])
        out_ref[:, hs] = y.astype(out_ref.dtype)


def _mlstm(proj, gates, i_bias, f_bias, a_norm, batch, seq):
    m = proj.shape[0]
    L = _tile(seq, MLSTM_CHUNK)
    nc = seq // L
    H = A_HEADS
    nw = A_WIDTH // A_WIDTH
    del nw
    bias = jnp.concatenate([i_bias, f_bias]).astype(F32)
    bias_col = bias.reshape(2 * H, 1)
    bias_row = jnp.zeros((1, LANES), F32).at[0, :2 * H].set(bias)
    gates_row = gates[:, :2 * H].reshape(batch, seq, 2 * H).transpose(0, 2, 1)
    qkvo = lambda c: pl.BlockSpec((L, A_WIDTH), lambda b, t: (b * nc + t, c))
    return pl.pallas_call(
        functools.partial(_mlstm_kernel, L=L),
        grid=(batch, nc),
        in_specs=[qkvo(0), qkvo(1), qkvo(2), qkvo(3),
                  pl.BlockSpec((None, 2 * H, L), lambda b, t: (b, 0, t)),
                  pl.BlockSpec((L, LANES), lambda b, t: (b * nc + t, 0)),
                  pl.BlockSpec((2 * H, 1), lambda b, t: (0, 0)),
                  pl.BlockSpec((1, LANES), lambda b, t: (0, 0)),
                  pl.BlockSpec((H, A_HEAD_DIM), lambda b, t: (0, 0))],
        out_specs=pl.BlockSpec((L, A_WIDTH), lambda b, t: (b * nc + t, 0)),
        out_shape=jax.ShapeDtypeStruct((m, A_WIDTH), BF16),
        scratch_shapes=[pltpu.VMEM((H, A_HEAD_DIM, A_HEAD_DIM), F32),
                        pltpu.VMEM((H, 1, A_HEAD_DIM), F32),
                        pltpu.VMEM((H, 1, LANES), F32)],
        compiler_params=_params(("parallel", "arbitrary"), 48),
        name="mlstm",
    )(proj, proj, proj, proj, gates_row, gates, bias_col, bias_row, a_norm)


def _sgu_kernel(u_ref, z_ref, sp_ref, sb_ref, bn_ref, o_ref, *, chunks):
    T, gd = B_CHUNK, B_GROUP_DIM
    row = lax.broadcasted_iota(jnp.int32, (T, T), 0)
    col = lax.broadcasted_iota(jnp.int32, (T, T), 1)
    lower = col <= row
    for g in range(B_GROUPS):
        gs = slice(g * gd, (g + 1) * gd)
        w = jnp.where(lower, sp_ref[g], 0.0).astype(BF16)
        bias = sb_ref[:, g:g + 1]
        bn = bn_ref[g:g + 1, :]
        for c in range(chunks):
            rs = slice(c * T, (c + 1) * T)
            zn = _rms(jax.nn.gelu(z_ref[rs, gs]), bn)
            zmix = _dot(w, zn.astype(BF16)) + bias
            o_ref[rs, gs] = (jax.nn.gelu(u_ref[rs, gs]) * zmix).astype(o_ref.dtype)


def _sgu(proj, spatial, spatial_bias, b_norm, seq, u_block, z_block):
    m = proj.shape[0]
    tm = _tile(seq, 512)
    chunks = tm // B_CHUNK
    return pl.pallas_call(
        functools.partial(_sgu_kernel, chunks=chunks),
        grid=(m // tm,),
        in_specs=[pl.BlockSpec((tm, B_WIDTH), lambda i: (i, u_block)),
                  pl.BlockSpec((tm, B_WIDTH), lambda i: (i, z_block)),
                  pl.BlockSpec((B_GROUPS, B_CHUNK, B_CHUNK), lambda i: (0, 0, 0)),
                  pl.BlockSpec((B_CHUNK, B_GROUPS), lambda i: (0, 0)),
                  pl.BlockSpec((B_GROUPS, B_GROUP_DIM), lambda i: (0, 0))],
        out_specs=pl.BlockSpec((tm, B_WIDTH), lambda i: (i, 0)),
        out_shape=jax.ShapeDtypeStruct((m, B_WIDTH), BF16),
        compiler_params=_params(("parallel",), 48),
        name="sgu",
    )(proj, proj, spatial, spatial_bias.T, b_norm)


def _even_mixer(h, g, w_in, i_bias, f_bias, a_norm, b_norm, spatial, spatial_bias, w_out, batch, seq):
    d = h.shape[1]
    n_main = 4 * A_WIDTH
    w_main = jnp.concatenate([w_in[:, :n_main], w_in[:, n_main + 2 * A_HEADS:]], axis=1).astype(BF16)
    w_gate = jnp.zeros((d, LANES), F32).at[:, :2 * A_HEADS].set(w_in[:, n_main:n_main + 2 * A_HEADS]).astype(BF16)
    proj = _norm_matmul(h, g, w_main)
    gates = _norm_matmul(h, g, w_gate, tn=LANES)
    h_a = _mlstm(proj, gates, i_bias, f_bias, a_norm, batch, seq)
    blk = n_main // B_WIDTH
    h_b = _sgu(proj, spatial, spatial_bias, b_norm, seq, blk, blk + 1)
    w_out = w_out.astype(BF16)
    return _matmul_res([h_a, h_b], [w_out[:A_WIDTH], w_out[A_WIDTH:]], h)


def _rope_tables(seq):
    pos = jnp.arange(seq, dtype=jnp.int32).astype(F32)

    def angles(d):
        inv = jnp.power(jnp.float32(ROPE_THETA), -jnp.arange(0, d, 2, dtype=F32) / d)
        ang = pos[:, None] * inv[None, :]
        return jnp.cos(ang), jnp.sin(ang)

    cos, sin = angles(C_HEAD_DIM)
    cq = jnp.concatenate([cos, cos], axis=1)
    sq = jnp.concatenate([-sin, sin], axis=1)
    cos, sin = angles(IDX_ROPE_DIM)
    half = IDX_ROPE_DIM // 2
    rest = IDX_DIM - IDX_ROPE_DIM
    zeros_h, zeros_r = jnp.zeros((seq, half), F32), jnp.zeros((seq, rest), F32)
    ci = jnp.concatenate([cos, cos, jnp.ones((seq, rest), F32)], axis=1)
    s_from_left = jnp.concatenate([zeros_h, sin, zeros_r], axis=1)
    s_from_right = jnp.concatenate([-sin, zeros_h, zeros_r], axis=1)
    rep = LANES // IDX_DIM
    return cq, sq, jnp.tile(ci, (1, rep)), jnp.tile(s_from_left, (1, rep)), jnp.tile(s_from_right, (1, rep))


def _odd_prep_kernel(main_ref, misc_ref, cq_ref, sq_ref, ci_ref, sl_ref, sr_ref, qn_ref, kn_ref, lg_ref, lb_ref,
                     q_out, k_out, v_out, iq_out, ik_out, iw_out):
    dh = C_HEAD_DIM
    cq, sq = cq_ref[...], sq_ref[...]
    ci, sl, sr = ci_ref[...], sl_ref[...], sr_ref[...]
    half = IDX_ROPE_DIM // 2

    def rope_full(x):
        return x * cq + pltpu.roll(x, dh // 2, 1) * sq

    def rope_idx(x):
        return x * ci + pltpu.roll(x, half, 1) * sl + pltpu.roll(x, LANES - half, 1) * sr

    qn = qn_ref[...] * (dh ** -0.5)
    for h in range(C_HEADS):
        hs = slice(h * dh, (h + 1) * dh)
        q_out[:, hs] = rope_full(_rms(main_ref[:, hs], qn)).astype(q_out.dtype)
    k0 = C_HEADS * dh
    for h in range(C_KV_HEADS):
        hs = slice(h * dh, (h + 1) * dh)
        src = slice(k0 + h * dh, k0 + (h + 1) * dh)
        k_out[:, hs] = rope_full(_rms(main_ref[:, src], kn_ref[...])).astype(k_out.dtype)
    v0 = k0 + C_KV_HEADS * dh
    v_out[...] = main_ref[:, v0:v0 + C_KV_HEADS * dh].astype(v_out.dtype)
    i0 = v0 + C_KV_HEADS * dh
    per_tile = LANES // IDX_DIM
    for t in range(IDX_HEADS // per_tile):
        y = rope_idx(main_ref[:, i0 + t * LANES:i0 + (t + 1) * LANES])
        for j in range(per_tile):
            iq_out[t * per_tile + j] = y[:, j * IDX_DIM:(j + 1) * IDX_DIM].astype(iq_out.dtype)

    misc = misc_ref[...]
    lane = lax.broadcasted_iota(jnp.int32, misc.shape, 1)
    is_key = lane < IDX_DIM
    mu = jnp.sum(jnp.where(is_key, misc, 0.0), axis=1, keepdims=True) / IDX_DIM
    cen = jnp.where(is_key, misc - mu, 0.0)
    var = jnp.sum(cen * cen, axis=1, keepdims=True) / IDX_DIM
    ikn = cen * lax.rsqrt(var + NORM_EPS) * lg_ref[...] + lb_ref[...]
    ik_out[...] = rope_idx(ikn)[:, :IDX_DIM].astype(ik_out.dtype)
    iw_out[...] = misc[:, IDX_DIM:IDX_DIM + IDX_HEADS] * (IDX_HEADS ** -0.5 * IDX_DIM ** -0.5)


def _odd_prep(main, misc, q_norm, k_norm, ln_g, ln_b, seq):
    m = main.shape[0]
    tm = _tile(seq, 256)
    ns = seq // tm
    tables = _rope_tables(seq)
    pad = lambda v: jnp.zeros((1, LANES), F32).at[0, :IDX_DIM].set(v)
    tab = pl.BlockSpec((tm, LANES), lambda i: (i % ns, 0))
    vec = pl.BlockSpec((1, LANES), lambda i: (0, 0))
    qw, kw = C_HEADS * C_HEAD_DIM, C_KV_HEADS * C_HEAD_DIM
    return pl.pallas_call(
        _odd_prep_kernel,
        grid=(m // tm,),
        in_specs=[pl.BlockSpec((tm, main.shape[1]), lambda i: (i, 0)),
                  pl.BlockSpec((tm, LANES), lambda i: (i, 0)),
                  tab, tab, tab, tab, tab, vec, vec, vec, vec],
        out_specs=[pl.BlockSpec((tm, qw), lambda i: (i, 0)),
                   pl.BlockSpec((tm, kw), lambda i: (i, 0)),
                   pl.BlockSpec((tm, kw), lambda i: (i, 0)),
                   pl.BlockSpec((IDX_HEADS, tm, IDX_DIM), lambda i: (0, i, 0)),
                   pl.BlockSpec((tm, IDX_DIM), lambda i: (i, 0)),
                   pl.BlockSpec((tm, IDX_HEADS), lambda i: (i, 0))],
        out_shape=[jax.ShapeDtypeStruct((m, qw), BF16),
                   jax.ShapeDtypeStruct((m, kw), BF16),
                   jax.ShapeDtypeStruct((m, kw), BF16),
                   jax.ShapeDtypeStruct((IDX_HEADS, m, IDX_DIM), BF16),
                   jax.ShapeDtypeStruct((m, IDX_DIM), BF16),
                   jax.ShapeDtypeStruct((m, IDX_HEADS), F32)],
        compiler_params=_params(("parallel",), 48),
        name="odd_prep",
    )(main, misc, *tables, q_norm.reshape(1, LANES), k_norm.reshape(1, LANES), pad(ln_g), pad(ln_b))


def _indexer_kernel(iq_ref, iw_ref, ik_ref, o_ref, key_ref, *, tq, tk, nkb, topk):
    qi = pl.program_id(1)
    n_causal = ((qi + 1) * tq + tk - 1) // tk
    q_pos = qi * tq + lax.broadcasted_iota(jnp.int32, (tq, 1), 0)
    k_off = lax.broadcasted_iota(jnp.int32, (1, tk), 1)
    iw = iw_ref[...]

    def causal(kb):
        return kb * tk + k_off <= q_pos

    def score_body(kb, carry):
        ikb = ik_ref[pl.ds(pl.multiple_of(kb * tk, tk), tk), :]
        acc = jnp.zeros((tq, tk), F32)
        for h in range(IDX_HEADS):
            acc = acc + iw[:, h:h + 1] * jnp.maximum(_dot_nt(iq_ref[h], ikb), 0.0)
        bits = lax.bitcast_convert_type(acc, jnp.int32)
        key = bits ^ ((bits >> 31) & jnp.int32(0x7FFFFFFF))
        key_ref[kb] = jnp.where(causal(kb), key, INT_MIN)
        return carry

    lax.fori_loop(0, n_causal, score_body, 0)

    def count_ge(cand):
        def body(kb, c):
            ge = (key_ref[kb] >= cand).astype(jnp.int32)
            for t in range(tk // LANES):
                c = c + ge[:, t * LANES:(t + 1) * LANES]
            return c
        c = lax.fori_loop(0, n_causal, body, jnp.zeros((tq, LANES), jnp.int32))
        return jnp.sum(c, axis=1, keepdims=True)

    def bisect(it, thr):
        cand = thr + jnp.left_shift(jnp.int32(1), 31 - it)
        return jnp.where(count_ge(cand) >= topk, cand, thr)

    thr = lax.fori_loop(0, 32, bisect, jnp.full((tq, 1), INT_MIN, jnp.int32))

    def write_body(kb, carry):
        sel = (key_ref[kb] >= thr) & causal(kb)
        o_ref[kb] = jnp.where(sel, 0.0, MASK_NEG).astype(o_ref.dtype)
        return carry

    lax.fori_loop(0, n_causal, write_body, 0)

    def fill_body(kb, carry):
        o_ref[kb] = jnp.full((tq, tk), MASK_NEG, o_ref.dtype)
        return carry

    lax.fori_loop(n_causal, nkb, fill_body, 0)


def _indexer(iq, iw, ik, batch, seq, tq, tk):
    nkb = seq // tk
    nq = seq // tq
    topk = min(TOPK_MAX, seq // 4)
    return pl.pallas_call(
        functools.partial(_indexer_kernel, tq=tq, tk=tk, nkb=nkb, topk=topk),
        grid=(batch, nq),
        in_specs=[pl.BlockSpec((IDX_HEADS, tq, IDX_DIM), lambda b, i: (0, b * nq + i, 0)),
                  pl.BlockSpec((tq, IDX_HEADS), lambda b, i: (b * nq + i, 0)),
                  pl.BlockSpec((seq, IDX_DIM), lambda b, i: (b, 0))],
        out_specs=pl.BlockSpec((None, nkb, tq, tk), lambda b, i: (b, 0, i, 0)),
        out_shape=jax.ShapeDtypeStruct((batch, nkb, seq, tk), BF16),
        scratch_shapes=[pltpu.VMEM((nkb, tq, tk), jnp.int32)],
        compiler_params=_params(("parallel", "parallel"), 56),
        name="indexer",
    )(iq, iw, ik)


def _attn_kernel(q_ref, k_ref, v_ref, b_ref, o_ref, acc_ref, m_ref, l_ref, *, tq, tk):
    dh = C_HEAD_DIM
    group = C_HEADS // C_KV_HEADS
    qi, kb = pl.program_id(1), pl.program_id(2)
    last_kb = (qi * tq + tq - 1) // tk

    @pl.when(kb == 0)
    def _():
        m_ref[...] = jnp.full_like(m_ref, MASK_NEG)
        l_ref[...] = jnp.zeros_like(l_ref)
        acc_ref[...] = jnp.zeros_like(acc_ref)

    @pl.when(kb <= last_kb)
    def _():
        bias = b_ref[...].astype(F32)
        for g in range(C_KV_HEADS):
            gs = slice(g * dh, (g + 1) * dh)
            kg, vg = k_ref[:, gs], v_ref[:, gs]
            for j in range(group):
                h = g * group + j
                hs = slice(h * dh, (h + 1) * dh)
                s = _dot_nt(q_ref[:, hs], kg) + bias
                m_old = m_ref[h]
                m_new = jnp.maximum(m_old, jnp.max(s, axis=1, keepdims=True))
                alpha = jnp.exp(m_old - m_new)
                p = jnp.exp(s - m_new[:, 0:1])
                l_ref[h] = alpha * l_ref[h] + jnp.sum(p, axis=1, keepdims=True)
                acc_ref[:, hs] = alpha * acc_ref[:, hs] + _dot(p.astype(BF16), vg)
                m_ref[h] = m_new

    @pl.when(kb == last_kb)
    def _():
        for h in range(C_HEADS):
            hs = slice(h * dh, (h + 1) * dh)
            o_ref[:, hs] = (acc_ref[:, hs] / l_ref[h]).astype(o_ref.dtype)


def _attention(q, k, v, mask, batch, seq, tq, tk):
    m = q.shape[0]
    nq, nkb = seq // tq, seq // tk
    qw, kw = C_HEADS * C_HEAD_DIM, C_KV_HEADS * C_HEAD_DIM
    clamp = lambda i, j: jnp.minimum(j, (i * tq + tq - 1) // tk)
    return pl.pallas_call(
        functools.partial(_attn_kernel, tq=tq, tk=tk),
        grid=(batch, nq, nkb),
        in_specs=[pl.BlockSpec((tq, qw), lambda b, i, j: (b * nq + i, 0)),
                  pl.BlockSpec((tk, kw), lambda b, i, j: (b * nkb + clamp(i, j), 0)),
                  pl.BlockSpec((tk, kw), lambda b, i, j: (b * nkb + clamp(i, j), 0)),
                  pl.BlockSpec((None, None, tq, tk), lambda b, i, j: (b, clamp(i, j), i, 0))],
        out_specs=pl.BlockSpec((tq, qw), lambda b, i, j: (b * nq + i, 0)),
        out_shape=jax.ShapeDtypeStruct((m, qw), BF16),
        scratch_shapes=[pltpu.VMEM((tq, qw), F32),
                        pltpu.VMEM((C_HEADS, tq, LANES), F32),
                        pltpu.VMEM((C_HEADS, tq, LANES), F32)],
        compiler_params=_params(("parallel", "parallel", "arbitrary"), 48),
        name="attention",
    )(q, k, v, mask)


def _odd_mixer(h, g, w_in, q_norm, k_norm, ln_g, ln_b, w_out, batch, seq):
    d = h.shape[1]
    n_main = (C_HEADS + 2 * C_KV_HEADS) * C_HEAD_DIM + IDX_HEADS * IDX_DIM
    n_misc = IDX_DIM + IDX_HEADS
    w_main = w_in[:, :n_main].astype(BF16)
    w_misc = jnp.zeros((d, LANES), F32).at[:, :n_misc].set(w_in[:, n_main:]).astype(BF16)
    main = _norm_matmul(h, g, w_main)
    misc = _norm_matmul(h, g, w_misc, tn=LANES)
    q, k, v, iq, ik, iw = _odd_prep(main, misc, q_norm, k_norm, ln_g, ln_b, seq)
    tq, tk = _tile(seq, 256), _tile(seq, 512)
    mask = _indexer(iq, iw, ik, batch, seq, tq, tk)
    att = _attention(q, k, v, mask, batch, seq, tq, tk)
    return _matmul_res([att], [w_out.astype(BF16)], h)


def kernel(x, mix_norm, even_w_in, even_i_bias, even_f_bias, even_a_norm, even_b_norm, even_spatial,
           even_spatial_bias, even_w_out, odd_w_in, odd_q_norm, odd_k_norm, odd_idx_ln_g, odd_idx_ln_b,
           odd_w_out, ffn_norm, ffn_w_up, ffn_conv_w, ffn_conv_b, ffn_w_down):
    batch, seq, d = x.shape
    depth = mix_norm.shape[0]
    h = x.reshape(batch * seq, d)
    for layer in range(depth):
        j = layer // 2
        if layer % 2 == 0:
            h = _even_mixer(h, mix_norm[layer], even_w_in[j], even_i_bias[j], even_f_bias[j], even_a_norm[j],
                            even_b_norm[j], even_spatial[j], even_spatial_bias[j], even_w_out[j], batch, seq)
        else:
            h = _odd_mixer(h, mix_norm[layer], odd_w_in[j], odd_q_norm[j], odd_k_norm[j], odd_idx_ln_g[j],
                           odd_idx_ln_b[j], odd_w_out[j], batch, seq)
        h = _conv_ffn(h, ffn_norm[layer], ffn_w_up[layer], ffn_conv_w[layer], ffn_conv_b[layer],
                      ffn_w_down[layer], seq)
    return h.reshape(batch, seq, d)
```

```python
import functools

import jax
import jax.numpy as jnp
from jax import lax
from jax.experimental import pallas as pl
from jax.experimental.pallas import tpu as pltpu

A_HEADS = 4
A_HEAD_DIM = 256
B_GROUPS = 4
B_GROUP_DIM = 256
B_CHUNK = 128
C_HEADS = 16
C_KV_HEADS = 4
C_HEAD_DIM = 128
IDX_HEADS = 16
IDX_DIM = 64
IDX_ROPE_DIM = 32
TOPK_MAX = 256
ROPE_THETA = 10000.0
CONV_WIDTH = 3
NORM_EPS = 1e-6

A_WIDTH = A_HEADS * A_HEAD_DIM
B_WIDTH = B_GROUPS * B_GROUP_DIM

LANES = 128
HALO = 16
MLSTM_CHUNK = 256
ATTN_ROWS = 32
IDX_ROWS = 128
MASK_NEG = -1e30
LOG2_E = 1.4426950408889634
INT_MIN = -(2 ** 31)

F32 = jnp.float32
BF16 = jnp.bfloat16


def _params(dims, vmem_mb):
    return pltpu.CompilerParams(dimension_semantics=dims, vmem_limit_bytes=vmem_mb * 1024 * 1024)


def _rms(x, g):
    return x * lax.rsqrt(jnp.mean(x * x, axis=-1, keepdims=True) + NORM_EPS) * g


def _log_sigmoid(x):
    return jnp.minimum(x, 0.0) - jnp.log1p(jnp.exp(-jnp.abs(x)))


def _dot(a, b):
    return jnp.dot(a, b, preferred_element_type=F32)


def _dot_nt(a, b):
    return lax.dot_general(a, b, (((1,), (1,)), ((), ())), preferred_element_type=F32)


def _dot_tn(a, b):
    return lax.dot_general(a, b, (((0,), (0,)), ((), ())), preferred_element_type=F32)


def _tile(n, pref):
    t = min(n, pref)
    assert n % t == 0, (n, pref)
    return t


def _norm_matmul_kernel(x_ref, g_ref, w_ref, o_ref, xn_ref):
    @pl.when(pl.program_id(1) == 0)
    def _():
        xn_ref[...] = _rms(x_ref[...], g_ref[...]).astype(BF16)

    o_ref[...] = _dot(xn_ref[...], w_ref[...]).astype(o_ref.dtype)


def _norm_matmul(x, g, w, out_dtype=F32, tm=1024, tn=1024):
    m, d = x.shape
    n = w.shape[1]
    tm, tn = _tile(m, tm), _tile(n, tn)
    return pl.pallas_call(
        _norm_matmul_kernel,
        grid=(m // tm, n // tn),
        in_specs=[pl.BlockSpec((tm, d), lambda i, j: (i, 0)),
                  pl.BlockSpec((1, d), lambda i, j: (0, 0)),
                  pl.BlockSpec((d, tn), lambda i, j: (0, j))],
        out_specs=pl.BlockSpec((tm, tn), lambda i, j: (i, j)),
        out_shape=jax.ShapeDtypeStruct((m, n), out_dtype),
        scratch_shapes=[pltpu.VMEM((tm, d), BF16)],
        compiler_params=_params(("parallel", "arbitrary"), 48),
        name="norm_matmul",
    )(x, g.reshape(1, d), w)


def _matmul_res_kernel(*refs, n_lhs):
    lhs, ws = refs[:n_lhs], refs[n_lhs:2 * n_lhs]
    res_ref, o_ref = refs[2 * n_lhs], refs[2 * n_lhs + 1]
    acc = res_ref[...]
    for a_ref, w_ref in zip(lhs, ws):
        acc = acc + _dot(a_ref[...], w_ref[...])
    o_ref[...] = acc


def _matmul_res(lhs_list, w_list, res, tm=1024, tn=1024):
    m, n = res.shape
    tm, tn = _tile(m, tm), _tile(n, tn)
    in_specs = ([pl.BlockSpec((tm, a.shape[1]), lambda i, j: (i, 0)) for a in lhs_list]
                + [pl.BlockSpec((w.shape[0], tn), lambda i, j: (0, j)) for w in w_list]
                + [pl.BlockSpec((tm, tn), lambda i, j: (i, j))])
    return pl.pallas_call(
        functools.partial(_matmul_res_kernel, n_lhs=len(lhs_list)),
        grid=(m // tm, n // tn),
        in_specs=in_specs,
        out_specs=pl.BlockSpec((tm, tn), lambda i, j: (i, j)),
        out_shape=jax.ShapeDtypeStruct((m, n), F32),
        compiler_params=_params(("parallel", "parallel"), 56),
        name="matmul_res",
    )(*lhs_list, *w_list, res)


def _ffn_up_kernel(x_ref, halo_ref, g_ref, wa_ref, wb_ref, cwa_ref, cwb_ref, cba_ref, cbb_ref, o_ref,
                   xn_ref, sa_ref, sb_ref, *, tm, seq):
    i = pl.program_id(0)

    @pl.when(pl.program_id(1) == 0)
    def _():
        g = g_ref[...]
        starts_sequence = (i * tm) % seq == 0
        halo = jnp.where(starts_sequence, 0.0, _rms(halo_ref[...], g))
        xn_ref[0:HALO, :] = halo.astype(BF16)
        xn_ref[HALO:, :] = _rms(x_ref[...], g).astype(BF16)

    xn = xn_ref[...]

    def conv(s_ref, cw_ref, cb_ref, cs):
        cw = cw_ref[:, cs]
        out = cb_ref[:, cs]
        for tap in range(CONV_WIDTH):
            lo = HALO - (CONV_WIDTH - 1) + tap
            out = out + cw[tap:tap + 1, :] * s_ref[lo:lo + tm, cs]
        return out

    tn = o_ref.shape[1]
    n_split = 2 if tn % (2 * LANES) == 0 else 1
    for part in range(n_split):
        cs = slice(part * (tn // n_split), (part + 1) * (tn // n_split))
        sa_ref[:, cs] = _dot(xn, wa_ref[:, cs])
        sb_ref[:, cs] = _dot(xn, wb_ref[:, cs])
        a = conv(sa_ref, cwa_ref, cba_ref, cs)
        b = conv(sb_ref, cwb_ref, cbb_ref, cs)
        o_ref[:, cs] = (a * jax.nn.sigmoid(a) * b).astype(o_ref.dtype)


def _ffn_up(x, g, w_up, conv_w, conv_b, seq, tm=1024, tn=512):
    m, d = x.shape
    f = w_up.shape[1] // 2
    tm, tn = _tile(seq, tm), _tile(f, tn)
    nj = f // tn
    conv_b = conv_b.reshape(1, 2 * f)
    return pl.pallas_call(
        functools.partial(_ffn_up_kernel, tm=tm, seq=seq),
        grid=(m // tm, nj),
        in_specs=[pl.BlockSpec((tm, d), lambda i, j: (i, 0)),
                  pl.BlockSpec((HALO, d), lambda i, j: (jnp.maximum(i * (tm // HALO) - 1, 0), 0)),
                  pl.BlockSpec((1, d), lambda i, j: (0, 0)),
                  pl.BlockSpec((d, tn), lambda i, j: (0, j)),
                  pl.BlockSpec((d, tn), lambda i, j: (0, j + nj)),
                  pl.BlockSpec((CONV_WIDTH, tn), lambda i, j: (0, j)),
                  pl.BlockSpec((CONV_WIDTH, tn), lambda i, j: (0, j + nj)),
                  pl.BlockSpec((1, tn), lambda i, j: (0, j)),
                  pl.BlockSpec((1, tn), lambda i, j: (0, j + nj))],
        out_specs=pl.BlockSpec((tm, tn), lambda i, j: (i, j)),
        out_shape=jax.ShapeDtypeStruct((m, f), BF16),
        scratch_shapes=[pltpu.VMEM((tm + HALO, d), BF16),
                        pltpu.VMEM((tm + HALO, tn), F32),
                        pltpu.VMEM((tm + HALO, tn), F32)],
        compiler_params=_params(("parallel", "arbitrary"), 48),
        name="ffn_up",
    )(x, x, g.reshape(1, d), w_up, w_up, conv_w, conv_w, conv_b, conv_b)


def _conv_ffn(h, g, w_up, conv_w, conv_b, w_down, seq):
    gated = _ffn_up(h, g, w_up.astype(BF16), conv_w, conv_b, seq)
    return _matmul_res([gated], [w_down.astype(BF16)], h, tn=512)


def _mlstm_kernel(q_ref, k_ref, v_ref, o_ref, grow_ref, gcol_ref, bcol_ref, brow_ref, an_ref, out_ref,
                  c_ref, n_ref, m_ref, *, L):
    H, dh = A_HEADS, A_HEAD_DIM

    @pl.when(pl.program_id(1) == 0)
    def _():
        c_ref[...] = jnp.zeros_like(c_ref)
        n_ref[...] = jnp.zeros_like(n_ref)
        m_ref[...] = jnp.zeros_like(m_ref)

    row = lax.broadcasted_iota(jnp.int32, (L, L), 0)
    col = lax.broadcasted_iota(jnp.int32, (L, L), 1)
    lower = col <= row
    upper = row <= col
    grow = grow_ref[...] + bcol_ref[...]
    gcol = gcol_ref[...] + brow_ref[...]

    for h in range(H):
        hs = slice(h * dh, (h + 1) * dh)
        i_row, lf_row = grow[h:h + 1, :], _log_sigmoid(grow[H + h:H + h + 1, :])
        i_col, lf_col = gcol[:, h:h + 1], _log_sigmoid(gcol[:, H + h:H + h + 1])
        b_col = jnp.sum(jnp.where(lower, lf_row, 0.0), axis=1, keepdims=True)
        b_row = jnp.sum(jnp.where(upper, lf_col, 0.0), axis=0, keepdims=True)
        a_row = i_row - b_row
        a_col = i_col - b_col
        m_prev = m_ref[h][:, 0:1]
        m_col = jnp.maximum(m_prev, jnp.max(jnp.where(lower, a_row, -jnp.inf), axis=1, keepdims=True))
        e = jnp.where(lower, jnp.exp(a_row - m_col), 0.0)

        q = q_ref[:, hs] * (dh ** -0.5)
        kf = k_ref[:, hs]
        qb, kb, vb = q.astype(BF16), kf.astype(BF16), v_ref[:, hs].astype(BF16)
        s = _dot_nt(qb, kb) * e
        w_inter = jnp.exp(m_prev - m_col)
        num = _dot(s.astype(BF16), vb) + w_inter * _dot(qb, c_ref[h].astype(BF16))
        den = jnp.sum(s, axis=1, keepdims=True) + w_inter * jnp.sum(q * n_ref[h], axis=1, keepdims=True)
        hh = num / jnp.maximum(jnp.abs(den), jnp.exp(-(b_col + m_col)))

        m_last = m_col[L - 1:L, :]
        decay = jnp.exp(m_prev - m_last)
        wk = jnp.exp(a_col - m_last) * kf
        c_ref[h] = decay * c_ref[h] + _dot_tn(wk.astype(BF16), vb)
        n_ref[h] = decay * n_ref[h] + jnp.sum(wk, axis=0, keepdims=True)
        m_ref[h] = jnp.broadcast_to(b_col[L - 1:L, :] + m_last, (1, LANES))

        y = _rms(hh, an_ref[h:h + 1, :]) * jax.nn.sigmoid(o_ref[:, hs])
        out_ref[:, hs] = y.astype(out_ref.dtype)


def _mlstm(proj, gates, i_bias, f_bias, a_norm, batch, seq):
    m = proj.shape[0]
    L = _tile(seq, MLSTM_CHUNK)
    nc = seq // L
    H = A_HEADS
    nw = A_WIDTH // A_WIDTH
    del nw
    bias = jnp.concatenate([i_bias, f_bias]).astype(F32)
    bias_col = bias.reshape(2 * H, 1)
    bias_row = jnp.zeros((1, LANES), F32).at[0, :2 * H].set(bias)
    gates_row = gates[:, :2 * H].reshape(batch, seq, 2 * H).transpose(0, 2, 1)
    qkvo = lambda c: pl.BlockSpec((L, A_WIDTH), lambda b, t: (b * nc + t, c))
    return pl.pallas_call(
        functools.partial(_mlstm_kernel, L=L),
        grid=(batch, nc),
        in_specs=[qkvo(0), qkvo(1), qkvo(2), qkvo(3),
                  pl.BlockSpec((None, 2 * H, L), lambda b, t: (b, 0, t)),
                  pl.BlockSpec((L, LANES), lambda b, t: (b * nc + t, 0)),
                  pl.BlockSpec((2 * H, 1), lambda b, t: (0, 0)),
                  pl.BlockSpec((1, LANES), lambda b, t: (0, 0)),
                  pl.BlockSpec((H, A_HEAD_DIM), lambda b, t: (0, 0))],
        out_specs=pl.BlockSpec((L, A_WIDTH), lambda b, t: (b * nc + t, 0)),
        out_shape=jax.ShapeDtypeStruct((m, A_WIDTH), BF16),
        scratch_shapes=[pltpu.VMEM((H, A_HEAD_DIM, A_HEAD_DIM), F32),
                        pltpu.VMEM((H, 1, A_HEAD_DIM), F32),
                        pltpu.VMEM((H, 1, LANES), F32)],
        compiler_params=_params(("parallel", "arbitrary"), 48),
        name="mlstm",
    )(proj, proj, proj, proj, gates_row, gates, bias_col, bias_row, a_norm)


def _sgu_kernel(u_ref, z_ref, sp_ref, sb_ref, bn_ref, o_ref, *, chunks):
    T, gd = B_CHUNK, B_GROUP_DIM
    row = lax.broadcasted_iota(jnp.int32, (T, T), 0)
    col = lax.broadcasted_iota(jnp.int32, (T, T), 1)
    lower = col <= row
    for g in range(B_GROUPS):
        gs = slice(g * gd, (g + 1) * gd)
        w = jnp.where(lower, sp_ref[g], 0.0).astype(BF16)
        bias = sb_ref[:, g:g + 1]
        bn = bn_ref[g:g + 1, :]
        for c in range(chunks):
            rs = slice(c * T, (c + 1) * T)
            zn = _rms(jax.nn.gelu(z_ref[rs, gs]), bn)
            zmix = _dot(w, zn.astype(BF16)) + bias
            o_ref[rs, gs] = (jax.nn.gelu(u_ref[rs, gs]) * zmix).astype(o_ref.dtype)


def _sgu(proj, spatial, spatial_bias, b_norm, seq, u_block, z_block):
    m = proj.shape[0]
    tm = _tile(seq, 512)
    chunks = tm // B_CHUNK
    return pl.pallas_call(
        functools.partial(_sgu_kernel, chunks=chunks),
        grid=(m // tm,),
        in_specs=[pl.BlockSpec((tm, B_WIDTH), lambda i: (i, u_block)),
                  pl.BlockSpec((tm, B_WIDTH), lambda i: (i, z_block)),
                  pl.BlockSpec((B_GROUPS, B_CHUNK, B_CHUNK), lambda i: (0, 0, 0)),
                  pl.BlockSpec((B_CHUNK, B_GROUPS), lambda i: (0, 0)),
                  pl.BlockSpec((B_GROUPS, B_GROUP_DIM), lambda i: (0, 0))],
        out_specs=pl.BlockSpec((tm, B_WIDTH), lambda i: (i, 0)),
        out_shape=jax.ShapeDtypeStruct((m, B_WIDTH), BF16),
        compiler_params=_params(("parallel",), 48),
        name="sgu",
    )(proj, proj, spatial, spatial_bias.T, b_norm)


def _even_mixer(h, g, w_in, i_bias, f_bias, a_norm, b_norm, spatial, spatial_bias, w_out, batch, seq):
    d = h.shape[1]
    n_main = 4 * A_WIDTH
    w_main = jnp.concatenate([w_in[:, :n_main], w_in[:, n_main + 2 * A_HEADS:]], axis=1).astype(BF16)
    w_gate = jnp.zeros((d, LANES), F32).at[:, :2 * A_HEADS].set(w_in[:, n_main:n_main + 2 * A_HEADS]).astype(BF16)
    proj = _norm_matmul(h, g, w_main)
    gates = _norm_matmul(h, g, w_gate, tn=LANES)
    h_a = _mlstm(proj, gates, i_bias, f_bias, a_norm, batch, seq)
    blk = n_main // B_WIDTH
    h_b = _sgu(proj, spatial, spatial_bias, b_norm, seq, blk, blk + 1)
    w_out = w_out.astype(BF16)
    return _matmul_res([h_a, h_b], [w_out[:A_WIDTH], w_out[A_WIDTH:]], h)


def _rope_tables(seq):
    pos = jnp.arange(seq, dtype=jnp.int32).astype(F32)

    def angles(d):
        inv = jnp.power(jnp.float32(ROPE_THETA), -jnp.arange(0, d, 2, dtype=F32) / d)
        ang = pos[:, None] * inv[None, :]
        return jnp.cos(ang), jnp.sin(ang)

    cos, sin = angles(C_HEAD_DIM)
    cq = jnp.concatenate([cos, cos], axis=1)
    sq = jnp.concatenate([-sin, sin], axis=1)
    cos, sin = angles(IDX_ROPE_DIM)
    half = IDX_ROPE_DIM // 2
    rest = IDX_DIM - IDX_ROPE_DIM
    zeros_h, zeros_r = jnp.zeros((seq, half), F32), jnp.zeros((seq, rest), F32)
    ci = jnp.concatenate([cos, cos, jnp.ones((seq, rest), F32)], axis=1)
    s_from_left = jnp.concatenate([zeros_h, sin, zeros_r], axis=1)
    s_from_right = jnp.concatenate([-sin, zeros_h, zeros_r], axis=1)
    rep = LANES // IDX_DIM
    return cq, sq, jnp.tile(ci, (1, rep)), jnp.tile(s_from_left, (1, rep)), jnp.tile(s_from_right, (1, rep))


def _odd_prep_kernel(main_ref, misc_ref, cq_ref, sq_ref, ci_ref, sl_ref, sr_ref, qn_ref, kn_ref, lg_ref, lb_ref,
                     q_out, k_out, v_out, iq_out, ik_out, iw_out):
    dh = C_HEAD_DIM
    cq, sq = cq_ref[...], sq_ref[...]
    ci, sl, sr = ci_ref[...], sl_ref[...], sr_ref[...]
    half = IDX_ROPE_DIM // 2

    def rope_full(x):
        return x * cq + pltpu.roll(x, dh // 2, 1) * sq

    def rope_idx(x):
        return x * ci + pltpu.roll(x, half, 1) * sl + pltpu.roll(x, LANES - half, 1) * sr

    qn = qn_ref[...] * (dh ** -0.5 * LOG2_E)
    for h in range(C_HEADS):
        hs = slice(h * dh, (h + 1) * dh)
        q_out[h] = rope_full(_rms(main_ref[:, hs], qn)).astype(q_out.dtype)
    k0 = C_HEADS * dh
    for h in range(C_KV_HEADS):
        hs = slice(h * dh, (h + 1) * dh)
        src = slice(k0 + h * dh, k0 + (h + 1) * dh)
        k_out[:, hs] = rope_full(_rms(main_ref[:, src], kn_ref[...])).astype(k_out.dtype)
    v0 = k0 + C_KV_HEADS * dh
    v_out[...] = main_ref[:, v0:v0 + C_KV_HEADS * dh].astype(v_out.dtype)
    i0 = v0 + C_KV_HEADS * dh
    per_tile = LANES // IDX_DIM
    for t in range(IDX_HEADS // per_tile):
        y = rope_idx(main_ref[:, i0 + t * LANES:i0 + (t + 1) * LANES])
        for j in range(per_tile):
            iq_out[t * per_tile + j] = y[:, j * IDX_DIM:(j + 1) * IDX_DIM].astype(iq_out.dtype)

    misc = misc_ref[...]
    lane = lax.broadcasted_iota(jnp.int32, misc.shape, 1)
    is_key = lane < IDX_DIM
    mu = jnp.sum(jnp.where(is_key, misc, 0.0), axis=1, keepdims=True) / IDX_DIM
    cen = jnp.where(is_key, misc - mu, 0.0)
    var = jnp.sum(cen * cen, axis=1, keepdims=True) / IDX_DIM
    ikn = cen * lax.rsqrt(var + NORM_EPS) * lg_ref[...] + lb_ref[...]
    ik_out[...] = rope_idx(ikn)[:, :IDX_DIM].astype(ik_out.dtype)
    iw_out[...] = misc[:, IDX_DIM:IDX_DIM + IDX_HEADS] * (IDX_HEADS ** -0.5 * IDX_DIM ** -0.5)


def _odd_prep(main, misc, q_norm, k_norm, ln_g, ln_b, seq):
    m = main.shape[0]
    tm = _tile(seq, 256)
    ns = seq // tm
    tables = _rope_tables(seq)
    pad = lambda v: jnp.zeros((1, LANES), F32).at[0, :IDX_DIM].set(v)
    tab = pl.BlockSpec((tm, LANES), lambda i: (i % ns, 0))
    vec = pl.BlockSpec((1, LANES), lambda i: (0, 0))
    qw, kw = C_HEADS * C_HEAD_DIM, C_KV_HEADS * C_HEAD_DIM
    return pl.pallas_call(
        _odd_prep_kernel,
        grid=(m // tm,),
        in_specs=[pl.BlockSpec((tm, main.shape[1]), lambda i: (i, 0)),
                  pl.BlockSpec((tm, LANES), lambda i: (i, 0)),
                  tab, tab, tab, tab, tab, vec, vec, vec, vec],
        out_specs=[pl.BlockSpec((C_HEADS, tm, C_HEAD_DIM), lambda i: (0, i, 0)),
                   pl.BlockSpec((tm, kw), lambda i: (i, 0)),
                   pl.BlockSpec((tm, kw), lambda i: (i, 0)),
                   pl.BlockSpec((IDX_HEADS, tm, IDX_DIM), lambda i: (0, i, 0)),
                   pl.BlockSpec((tm, IDX_DIM), lambda i: (i, 0)),
                   pl.BlockSpec((tm, IDX_HEADS), lambda i: (i, 0))],
        out_shape=[jax.ShapeDtypeStruct((C_HEADS, m, C_HEAD_DIM), BF16),
                   jax.ShapeDtypeStruct((m, kw), BF16),
                   jax.ShapeDtypeStruct((m, kw), BF16),
                   jax.ShapeDtypeStruct((IDX_HEADS, m, IDX_DIM), BF16),
                   jax.ShapeDtypeStruct((m, IDX_DIM), BF16),
                   jax.ShapeDtypeStruct((m, IDX_HEADS), F32)],
        compiler_params=_params(("parallel",), 48),
        name="odd_prep",
    )(main, misc, *tables, q_norm.reshape(1, LANES), k_norm.reshape(1, LANES), pad(ln_g), pad(ln_b))


def _indexer_kernel(iq_ref, iw_ref, ik_ref, o_ref, key_ref, thr_ref, lim_ref, *, tq, tk, nkb, topk, pos_bits):
    qi = pl.program_id(1)
    n_causal = ((qi + 1) * tq + tk - 1) // tk
    q_pos = qi * tq + lax.broadcasted_iota(jnp.int32, (tq, 1), 0)
    q_pos_rep = qi * tq + lax.broadcasted_iota(jnp.int32, (tq, LANES), 0)
    k_off = lax.broadcasted_iota(jnp.int32, (1, tk), 1)
    iw = iw_ref[...]

    def causal(kb):
        return kb * tk + k_off <= q_pos

    def score_body(kb, carry):
        ikb = ik_ref[pl.ds(pl.multiple_of(kb * tk, tk), tk), :]
        acc = jnp.zeros((tq, tk), F32)
        for h in range(IDX_HEADS):
            acc = acc + iw[:, h:h + 1] * jnp.maximum(_dot_nt(iq_ref[h], ikb), 0.0)
        bits = lax.bitcast_convert_type(acc, jnp.int32)
        key = bits ^ ((bits >> 31) & jnp.int32(0x7FFFFFFF))
        key_ref[kb] = jnp.where(causal(kb), key, INT_MIN)
        return carry

    lax.fori_loop(0, n_causal, score_body, 0)

    n_tiles = tk // LANES
    lane_off = lax.broadcasted_iota(jnp.int32, (1, LANES), 1)

    def row_block(rb, carry):
        r0 = pl.multiple_of(rb * IDX_ROWS, IDX_ROWS)
        n_blk = (qi * tq + r0 + IDX_ROWS + tk - 1) // tk

        def count(pred):
            def body(kb, c):
                for t in range(n_tiles):
                    key = key_ref[kb, pl.ds(r0, IDX_ROWS), t * LANES:(t + 1) * LANES]
                    c = c + pred(key, kb * tk + t * LANES + lane_off).astype(jnp.int32)
                return c
            c = lax.fori_loop(0, n_blk, body, jnp.zeros((IDX_ROWS, LANES), jnp.int32))
            return jnp.broadcast_to(jnp.sum(c, axis=1, keepdims=True), (IDX_ROWS, LANES))

        def bisect(it, thr):
            cand = thr + jnp.left_shift(jnp.int32(1), 31 - it)
            return jnp.where(count(lambda key, pos: key >= cand) >= topk, cand, thr)

        thr = lax.fori_loop(0, 32, bisect, jnp.full((IDX_ROWS, LANES), INT_MIN, jnp.int32))

        def tie_limit():
            keep = topk - count(lambda key, pos: key > thr)

            def step(it, below):
                cand = below + jnp.left_shift(jnp.int32(1), pos_bits - 1 - it)
                n_le = count(lambda key, pos: (key == thr) & (pos <= cand))
                return jnp.where(n_le < keep, cand, below)

            below = lax.fori_loop(0, pos_bits, step, jnp.full((IDX_ROWS, LANES), -1, jnp.int32))
            return below + 1

        n_ge = count(lambda key, pos: key >= thr)
        has_excess_ties = jnp.max(n_ge.astype(F32)) > topk
        lim = lax.cond(has_excess_ties, tie_limit, lambda: jnp.full((IDX_ROWS, LANES), 2 ** pos_bits, jnp.int32))
        thr_ref[pl.ds(r0, IDX_ROWS), :] = thr
        lim_ref[pl.ds(r0, IDX_ROWS), :] = lim
        return carry

    lax.fori_loop(0, tq // IDX_ROWS, row_block, 0)

    def write_body(kb, carry):
        thr, lim = thr_ref[...], lim_ref[...]
        for t in range(n_tiles):
            ls = slice(t * LANES, (t + 1) * LANES)
            key = key_ref[kb, :, ls]
            pos = kb * tk + t * LANES + lane_off
            sel = ((key > thr) | ((key == thr) & (pos <= lim))) & (pos <= q_pos_rep)
            o_ref[kb, :, ls] = jnp.where(sel, 0.0, MASK_NEG).astype(o_ref.dtype)
        return carry

    lax.fori_loop(0, n_causal, write_body, 0)

    def fill_body(kb, carry):
        o_ref[kb] = jnp.full((tq, tk), MASK_NEG, o_ref.dtype)
        return carry

    lax.fori_loop(n_causal, nkb, fill_body, 0)


def _indexer(iq, iw, ik, batch, seq, tq, tk):
    nkb = seq // tk
    nq = seq // tq
    topk = min(TOPK_MAX, seq // 4)
    return pl.pallas_call(
        functools.partial(_indexer_kernel, tq=tq, tk=tk, nkb=nkb, topk=topk,
                          pos_bits=max(1, (seq - 1).bit_length())),
        grid=(batch, nq),
        in_specs=[pl.BlockSpec((IDX_HEADS, tq, IDX_DIM), lambda b, i: (0, b * nq + i, 0)),
                  pl.BlockSpec((tq, IDX_HEADS), lambda b, i: (b * nq + i, 0)),
                  pl.BlockSpec((seq, IDX_DIM), lambda b, i: (b, 0))],
        out_specs=pl.BlockSpec((None, nkb, tq, tk), lambda b, i: (b, 0, i, 0)),
        out_shape=jax.ShapeDtypeStruct((batch, nkb, seq, tk), BF16),
        scratch_shapes=[pltpu.VMEM((nkb, tq, tk), jnp.int32),
                        pltpu.VMEM((tq, LANES), jnp.int32),
                        pltpu.VMEM((tq, LANES), jnp.int32)],
        compiler_params=_params(("parallel", "parallel"), 56),
        name="indexer",
    )(iq, iw, ik)


def _attn_kernel(q_ref, k_ref, v_ref, b_ref, o_ref, acc_ref, m_ref, l_ref, s_ref, p_ref, bias_ref, *, tq, tk):
    dh = C_HEAD_DIM
    group = C_HEADS // C_KV_HEADS
    qi, kb = pl.program_id(1), pl.program_id(2)
    last_kb = (qi * tq + tq - 1) // tk

    @pl.when(kb == 0)
    def _():
        m_ref[...] = jnp.full_like(m_ref, MASK_NEG)
        l_ref[...] = jnp.zeros_like(l_ref)
        acc_ref[...] = jnp.zeros_like(acc_ref)

    @pl.when(kb <= last_kb)
    def _():
        rows = group * tq
        bias_ref[...] = b_ref[...].astype(F32)

        def logits(g):
            qg = q_ref[g * group:(g + 1) * group].reshape(rows, dh)
            s_ref[g % 2] = _dot_nt(qg, k_ref[:, g * dh:(g + 1) * dh])

        logits(0)
        for g in range(C_KV_HEADS):
            if g + 1 < C_KV_HEADS:
                logits(g + 1)
            for c in range(rows // ATTN_ROWS):
                rs = slice(c * ATTN_ROWS, (c + 1) * ATTN_ROWS)
                brs = slice((c * ATTN_ROWS) % tq, (c * ATTN_ROWS) % tq + ATTN_ROWS)
                tiles = [s_ref[g % 2, rs, t * LANES:(t + 1) * LANES] + bias_ref[brs, t * LANES:(t + 1) * LANES]
                         for t in range(tk // LANES)]
                tile_max = functools.reduce(jnp.maximum, tiles)
                m_old = m_ref[g, rs, :]
                m_new = jnp.maximum(m_old, jnp.max(tile_max, axis=1, keepdims=True))
                alpha = jnp.exp2(m_old - m_new)
                ps = [jnp.exp2(t - m_new) for t in tiles]
                l_ref[g, rs, :] = alpha * l_ref[g, rs, :] + jnp.sum(functools.reduce(jnp.add, ps), axis=1,
                                                                  keepdims=True)
                m_ref[g, rs, :] = m_new
                acc_ref[g, rs, :] = alpha * acc_ref[g, rs, :]
                for t, p in enumerate(ps):
                    p_ref[g % 2, rs, t * LANES:(t + 1) * LANES] = p.astype(BF16)
            acc_ref[g] += _dot(p_ref[g % 2], v_ref[:, g * dh:(g + 1) * dh])

    @pl.when(kb == last_kb)
    def _():
        for g in range(C_KV_HEADS):
            out = acc_ref[g] / l_ref[g]
            for j in range(group):
                h = g * group + j
                o_ref[:, h * dh:(h + 1) * dh] = out[j * tq:(j + 1) * tq].astype(o_ref.dtype)


def _attention(q, k, v, mask, batch, seq, tq, tk):
    m = k.shape[0]
    nq, nkb = seq // tq, seq // tk
    qw, kw = C_HEADS * C_HEAD_DIM, C_KV_HEADS * C_HEAD_DIM
    rows = (C_HEADS // C_KV_HEADS) * tq
    clamp = lambda i, j: jnp.minimum(j, (i * tq + tq - 1) // tk)
    return pl.pallas_call(
        functools.partial(_attn_kernel, tq=tq, tk=tk),
        grid=(batch, nq, nkb),
        in_specs=[pl.BlockSpec((C_HEADS, tq, C_HEAD_DIM), lambda b, i, j: (0, b * nq + i, 0)),
                  pl.BlockSpec((tk, kw), lambda b, i, j: (b * nkb + clamp(i, j), 0)),
                  pl.BlockSpec((tk, kw), lambda b, i, j: (b * nkb + clamp(i, j), 0)),
                  pl.BlockSpec((None, None, tq, tk), lambda b, i, j: (b, clamp(i, j), i, 0))],
        out_specs=pl.BlockSpec((tq, qw), lambda b, i, j: (b * nq + i, 0)),
        out_shape=jax.ShapeDtypeStruct((m, qw), BF16),
        scratch_shapes=[pltpu.VMEM((C_KV_HEADS, rows, C_HEAD_DIM), F32),
                        pltpu.VMEM((C_KV_HEADS, rows, LANES), F32),
                        pltpu.VMEM((C_KV_HEADS, rows, LANES), F32),
                        pltpu.VMEM((2, rows, tk), F32),
                        pltpu.VMEM((2, rows, tk), BF16),
                        pltpu.VMEM((tq, tk), F32)],
        compiler_params=_params(("parallel", "parallel", "arbitrary"), 48),
        name="attention",
    )(q, k, v, mask)


def _odd_mixer(h, g, w_in, q_norm, k_norm, ln_g, ln_b, w_out, batch, seq):
    d = h.shape[1]
    n_main = (C_HEADS + 2 * C_KV_HEADS) * C_HEAD_DIM + IDX_HEADS * IDX_DIM
    n_misc = IDX_DIM + IDX_HEADS
    w_main = w_in[:, :n_main].astype(BF16)
    w_misc = jnp.zeros((d, LANES), F32).at[:, :n_misc].set(w_in[:, n_main:]).astype(BF16)
    main = _norm_matmul(h, g, w_main)
    misc = _norm_matmul(h, g, w_misc, tn=LANES)
    q, k, v, iq, ik, iw = _odd_prep(main, misc, q_norm, k_norm, ln_g, ln_b, seq)
    tq, tk = _tile(seq, 256), _tile(seq, 512)
    mask = _indexer(iq, iw, ik, batch, seq, tq, tk)
    att = _attention(q, k, v, mask, batch, seq, tq, tk)
    return _matmul_res([att], [w_out.astype(BF16)], h)


def kernel(x, mix_norm, even_w_in, even_i_bias, even_f_bias, even_a_norm, even_b_norm, even_spatial,
           even_spatial_bias, even_w_out, odd_w_in, odd_q_norm, odd_k_norm, odd_idx_ln_g, odd_idx_ln_b,
           odd_w_out, ffn_norm, ffn_w_up, ffn_conv_w, ffn_conv_b, ffn_w_down):
    batch, seq, d = x.shape
    depth = mix_norm.shape[0]
    h = x.reshape(batch * seq, d)
    for layer in range(depth):
        j = layer // 2
        if layer % 2 == 0:
            h = _even_mixer(h, mix_norm[layer], even_w_in[j], even_i_bias[j], even_f_bias[j], even_a_norm[j],
                            even_b_norm[j], even_spatial[j], even_spatial_bias[j], even_w_out[j], batch, seq)
        else:
            h = _odd_mixer(h, mix_norm[layer], odd_w_in[j], odd_q_norm[j], odd_k_norm[j], odd_idx_ln_g[j],
                           odd_idx_ln_b[j], odd_w_out[j], batch, seq)
        h = _conv_ffn(h, ffn_norm[layer], ffn_w_up[layer], ffn_conv_w[layer], ffn_conv_b[layer],
                      ffn_w_down[layer], seq)
    return h.reshape(batch, seq, d)
```

```python
import functools

import jax
import jax.numpy as jnp
from jax import lax
from jax.experimental import pallas as pl
from jax.experimental.pallas import tpu as pltpu

A_HEADS = 4
A_HEAD_DIM = 256
B_GROUPS = 4
B_GROUP_DIM = 256
B_CHUNK = 128
C_HEADS = 16
C_KV_HEADS = 4
C_HEAD_DIM = 128
IDX_HEADS = 16
IDX_DIM = 64
IDX_ROPE_DIM = 32
TOPK_MAX = 256
ROPE_THETA = 10000.0
CONV_WIDTH = 3
NORM_EPS = 1e-6

A_WIDTH = A_HEADS * A_HEAD_DIM
B_WIDTH = B_GROUPS * B_GROUP_DIM

LANES = 128
HALO = 16
MLSTM_CHUNK = 256
ATTN_ROWS = 32
FFN_ROWS = 32
SUBLANES = 8
IDX_KEYS = 256
IDX_ACCS = 4
MASK_NEG = -1e30
LOG2_E = 1.4426950408889634
INT_MIN = -(2 ** 31)

F32 = jnp.float32
BF16 = jnp.bfloat16


def _params(dims, vmem_mb):
    return pltpu.CompilerParams(dimension_semantics=dims, vmem_limit_bytes=vmem_mb * 1024 * 1024)


def _rms(x, g):
    return x * lax.rsqrt(jnp.mean(x * x, axis=-1, keepdims=True) + NORM_EPS) * g


def _log_sigmoid(x):
    return jnp.minimum(x, 0.0) - jnp.log1p(jnp.exp(-jnp.abs(x)))


def _dot(a, b):
    return jnp.dot(a, b, preferred_element_type=F32)


def _dot_nt(a, b):
    return lax.dot_general(a, b, (((1,), (1,)), ((), ())), preferred_element_type=F32)


def _dot_tn(a, b):
    return lax.dot_general(a, b, (((0,), (0,)), ((), ())), preferred_element_type=F32)


def _tile(n, pref):
    t = min(n, pref)
    assert n % t == 0, (n, pref)
    return t


def _norm_matmul_kernel(x_ref, g_ref, w_ref, o_ref, xn_ref):
    @pl.when(pl.program_id(1) == 0)
    def _():
        xn_ref[...] = _rms(x_ref[...], g_ref[...]).astype(BF16)

    o_ref[...] = _dot(xn_ref[...], w_ref[...]).astype(o_ref.dtype)


def _norm_matmul(x, g, w, out_dtype=F32, tm=1024, tn=1024):
    m, d = x.shape
    n = w.shape[1]
    tm, tn = _tile(m, tm), _tile(n, tn)
    return pl.pallas_call(
        _norm_matmul_kernel,
        grid=(m // tm, n // tn),
        in_specs=[pl.BlockSpec((tm, d), lambda i, j: (i, 0)),
                  pl.BlockSpec((1, d), lambda i, j: (0, 0)),
                  pl.BlockSpec((d, tn), lambda i, j: (0, j))],
        out_specs=pl.BlockSpec((tm, tn), lambda i, j: (i, j)),
        out_shape=jax.ShapeDtypeStruct((m, n), out_dtype),
        scratch_shapes=[pltpu.VMEM((tm, d), BF16)],
        compiler_params=_params(("parallel", "arbitrary"), 48),
        name="norm_matmul",
    )(x, g.reshape(1, d), w)


def _matmul_res_kernel(*refs, n_lhs):
    lhs, ws = refs[:n_lhs], refs[n_lhs:2 * n_lhs]
    res_ref, o_ref = refs[2 * n_lhs], refs[2 * n_lhs + 1]
    acc = res_ref[...]
    for a_ref, w_ref in zip(lhs, ws):
        acc = acc + _dot(a_ref[...], w_ref[...])
    o_ref[...] = acc


def _matmul_res(lhs_list, w_list, res, tm=1024, tn=1024):
    m, n = res.shape
    tm, tn = _tile(m, tm), _tile(n, tn)
    in_specs = ([pl.BlockSpec((tm, a.shape[1]), lambda i, j: (i, 0)) for a in lhs_list]
                + [pl.BlockSpec((w.shape[0], tn), lambda i, j: (0, j)) for w in w_list]
                + [pl.BlockSpec((tm, tn), lambda i, j: (i, j))])
    return pl.pallas_call(
        functools.partial(_matmul_res_kernel, n_lhs=len(lhs_list)),
        grid=(m // tm, n // tn),
        in_specs=in_specs,
        out_specs=pl.BlockSpec((tm, tn), lambda i, j: (i, j)),
        out_shape=jax.ShapeDtypeStruct((m, n), F32),
        compiler_params=_params(("parallel", "parallel"), 56),
        name="matmul_res",
    )(*lhs_list, *w_list, res)


def _ffn_up_kernel(x_ref, halo_ref, g_ref, wa_ref, wb_ref, cwa_ref, cwb_ref, o_ref, xn_ref, sa_ref, sb_ref,
                   *, tm, seq, nj, n_tiles):
    s = pl.program_id(0)
    tile = jnp.minimum(s, n_tiles - 1)
    i = tile // nj

    @pl.when(s == 0)
    def _():
        sa_ref[1] = jnp.zeros(sa_ref.shape[1:], F32)
        sb_ref[1] = jnp.zeros(sb_ref.shape[1:], F32)

    @pl.when((tile % nj == 0) & (s < n_tiles))
    def _():
        g = g_ref[...]
        starts_sequence = (i * tm) % seq == 0
        halo = jnp.where(starts_sequence, 0.0, _rms(halo_ref[...], g))
        xn_ref[0:HALO, :] = halo.astype(BF16)
        xn_ref[HALO:, :] = _rms(x_ref[...], g).astype(BF16)

    def conv(s_ref, slot, cw_ref, r0):
        out = cw_ref[CONV_WIDTH:CONV_WIDTH + 1, :]
        for tap in range(CONV_WIDTH):
            lo = HALO - (CONV_WIDTH - 1) + tap + r0
            out = out + cw_ref[tap:tap + 1, :] * s_ref[slot, lo:lo + FFN_ROWS, :]
        return out

    for slot in range(2):
        @pl.when(s % 2 == slot)
        def _():
            for r0 in range(0, tm, FFN_ROWS):
                a = conv(sa_ref, 1 - slot, cwa_ref, r0)
                b = conv(sb_ref, 1 - slot, cwb_ref, r0)
                o_ref[r0:r0 + FFN_ROWS, :] = (a * jax.nn.sigmoid(a) * b).astype(o_ref.dtype)
            xn = xn_ref[...]
            sa_ref[slot] = _dot(xn, wa_ref[...])
            sb_ref[slot] = _dot(xn, wb_ref[...])


def _ffn_up(x, g, w_up, conv_w, conv_b, seq, tm=1024, tn=512):
    m, d = x.shape
    f = w_up.shape[1] // 2
    tm, tn = _tile(seq, tm), _tile(f, tn)
    nj = f // tn
    n_tiles = (m // tm) * nj
    conv_wb = jnp.concatenate([conv_w, conv_b.reshape(1, 2 * f)], axis=0)
    cur = lambda s: jnp.minimum(s, n_tiles - 1)
    prev = lambda s: jnp.maximum(s - 1, 0)
    return pl.pallas_call(
        functools.partial(_ffn_up_kernel, tm=tm, seq=seq, nj=nj, n_tiles=n_tiles),
        grid=(n_tiles + 1,),
        in_specs=[pl.BlockSpec((tm, d), lambda s: (cur(s) // nj, 0)),
                  pl.BlockSpec((HALO, d), lambda s: (jnp.maximum((cur(s) // nj) * (tm // HALO) - 1, 0), 0)),
                  pl.BlockSpec((1, d), lambda s: (0, 0)),
                  pl.BlockSpec((d, tn), lambda s: (0, cur(s) % nj)),
                  pl.BlockSpec((d, tn), lambda s: (0, cur(s) % nj + nj)),
                  pl.BlockSpec((CONV_WIDTH + 1, tn), lambda s: (0, prev(s) % nj)),
                  pl.BlockSpec((CONV_WIDTH + 1, tn), lambda s: (0, prev(s) % nj + nj))],
        out_specs=pl.BlockSpec((tm, tn), lambda s: (prev(s) // nj, prev(s) % nj)),
        out_shape=jax.ShapeDtypeStruct((m, f), BF16),
        scratch_shapes=[pltpu.VMEM((tm + HALO, d), BF16),
                        pltpu.VMEM((2, tm + HALO, tn), F32),
                        pltpu.VMEM((2, tm + HALO, tn), F32)],
        compiler_params=_params(("arbitrary",), 48),
        name="ffn_up",
    )(x, x, g.reshape(1, d), w_up, w_up, conv_wb, conv_wb)


def _conv_ffn(h, g, w_up, conv_w, conv_b, w_down, seq):
    gated = _ffn_up(h, g, w_up.astype(BF16), conv_w, conv_b, seq)
    return _matmul_res([gated], [w_down.astype(BF16)], h, tn=512)


def _mlstm_kernel(q_ref, k_ref, v_ref, o_ref, grow_ref, gcol_ref, bcol_ref, brow_ref, an_ref, out_ref,
                  c_ref, n_ref, m_ref, *, L):
    H, dh = A_HEADS, A_HEAD_DIM

    @pl.when(pl.program_id(1) == 0)
    def _():
        c_ref[...] = jnp.zeros_like(c_ref)
        n_ref[...] = jnp.zeros_like(n_ref)
        m_ref[...] = jnp.zeros_like(m_ref)

    row = lax.broadcasted_iota(jnp.int32, (L, L), 0)
    col = lax.broadcasted_iota(jnp.int32, (L, L), 1)
    lower = col <= row
    upper = row <= col
    grow = grow_ref[...] + bcol_ref[...]
    gcol = gcol_ref[...] + brow_ref[...]

    for h in range(H):
        hs = slice(h * dh, (h + 1) * dh)
        i_row, lf_row = grow[h:h + 1, :], _log_sigmoid(grow[H + h:H + h + 1, :])
        i_col, lf_col = gcol[:, h:h + 1], _log_sigmoid(gcol[:, H + h:H + h + 1])
        b_col = jnp.sum(jnp.where(lower, lf_row, 0.0), axis=1, keepdims=True)
        b_row = jnp.sum(jnp.where(upper, lf_col, 0.0), axis=0, keepdims=True)
        a_row = i_row - b_row
        a_col = i_col - b_col
        m_prev = m_ref[h][:, 0:1]
        m_col = jnp.maximum(m_prev, jnp.max(jnp.where(lower, a_row, -jnp.inf), axis=1, keepdims=True))
        e = jnp.where(lower, jnp.exp(a_row - m_col), 0.0)

        q = q_ref[:, hs] * (dh ** -0.5)
        kf = k_ref[:, hs]
        qb, kb, vb = q.astype(BF16), kf.astype(BF16), v_ref[:, hs].astype(BF16)
        s = _dot_nt(qb, kb) * e
        w_inter = jnp.exp(m_prev - m_col)
        num = _dot(s.astype(BF16), vb) + w_inter * _dot(qb, c_ref[h].astype(BF16))
        den = jnp.sum(s, axis=1, keepdims=True) + w_inter * jnp.sum(q * n_ref[h], axis=1, keepdims=True)
        hh = num / jnp.maximum(jnp.abs(den), jnp.exp(-(b_col + m_col)))

        m_last = m_col[L - 1:L, :]
        decay = jnp.exp(m_prev - m_last)
        wk = jnp.exp(a_col - m_last) * kf
        c_ref[h] = decay * c_ref[h] + _dot_tn(wk.astype(BF16), vb)
        n_ref[h] = decay * n_ref[h] + jnp.sum(wk, axis=0, keepdims=True)
        m_ref[h] = jnp.broadcast_to(b_col[L - 1:L, :] + m_last, (1, LANES))

        y = _rms(hh, an_ref[h:h + 1, :]) * jax.nn.sigmoid(o_ref[:, hs])
        out_ref[:, hs] = y.astype(out_ref.dtype)


def _mlstm(proj, gates, i_bias, f_bias, a_norm, batch, seq):
    m = proj.shape[0]
    L = _tile(seq, MLSTM_CHUNK)
    nc = seq // L
    H = A_HEADS
    nw = A_WIDTH // A_WIDTH
    del nw
    bias = jnp.concatenate([i_bias, f_bias]).astype(F32)
    bias_col = bias.reshape(2 * H, 1)
    bias_row = jnp.zeros((1, LANES), F32).at[0, :2 * H].set(bias)
    gates_row = gates[:, :2 * H].reshape(batch, seq, 2 * H).transpose(0, 2, 1)
    qkvo = lambda c: pl.BlockSpec((L, A_WIDTH), lambda b, t: (b * nc + t, c))
    return pl.pallas_call(
        functools.partial(_mlstm_kernel, L=L),
        grid=(batch, nc),
        in_specs=[qkvo(0), qkvo(1), qkvo(2), qkvo(3),
                  pl.BlockSpec((None, 2 * H, L), lambda b, t: (b, 0, t)),
                  pl.BlockSpec((L, LANES), lambda b, t: (b * nc + t, 0)),
                  pl.BlockSpec((2 * H, 1), lambda b, t: (0, 0)),
                  pl.BlockSpec((1, LANES), lambda b, t: (0, 0)),
                  pl.BlockSpec((H, A_HEAD_DIM), lambda b, t: (0, 0))],
        out_specs=pl.BlockSpec((L, A_WIDTH), lambda b, t: (b * nc + t, 0)),
        out_shape=jax.ShapeDtypeStruct((m, A_WIDTH), BF16),
        scratch_shapes=[pltpu.VMEM((H, A_HEAD_DIM, A_HEAD_DIM), F32),
                        pltpu.VMEM((H, 1, A_HEAD_DIM), F32),
                        pltpu.VMEM((H, 1, LANES), F32)],
        compiler_params=_params(("parallel", "arbitrary"), 48),
        name="mlstm",
    )(proj, proj, proj, proj, gates_row, gates, bias_col, bias_row, a_norm)


def _sgu_kernel(u_ref, z_ref, sp_ref, sb_ref, bn_ref, o_ref, *, chunks):
    T, gd = B_CHUNK, B_GROUP_DIM
    row = lax.broadcasted_iota(jnp.int32, (T, T), 0)
    col = lax.broadcasted_iota(jnp.int32, (T, T), 1)
    lower = col <= row
    for g in range(B_GROUPS):
        gs = slice(g * gd, (g + 1) * gd)
        w = jnp.where(lower, sp_ref[g], 0.0).astype(BF16)
        bias = sb_ref[:, g:g + 1]
        bn = bn_ref[g:g + 1, :]
        for c in range(chunks):
            rs = slice(c * T, (c + 1) * T)
            zn = _rms(jax.nn.gelu(z_ref[rs, gs]), bn)
            zmix = _dot(w, zn.astype(BF16)) + bias
            o_ref[rs, gs] = (jax.nn.gelu(u_ref[rs, gs]) * zmix).astype(o_ref.dtype)


def _sgu(proj, spatial, spatial_bias, b_norm, seq, u_block, z_block):
    m = proj.shape[0]
    tm = _tile(seq, 512)
    chunks = tm // B_CHUNK
    return pl.pallas_call(
        functools.partial(_sgu_kernel, chunks=chunks),
        grid=(m // tm,),
        in_specs=[pl.BlockSpec((tm, B_WIDTH), lambda i: (i, u_block)),
                  pl.BlockSpec((tm, B_WIDTH), lambda i: (i, z_block)),
                  pl.BlockSpec((B_GROUPS, B_CHUNK, B_CHUNK), lambda i: (0, 0, 0)),
                  pl.BlockSpec((B_CHUNK, B_GROUPS), lambda i: (0, 0)),
                  pl.BlockSpec((B_GROUPS, B_GROUP_DIM), lambda i: (0, 0))],
        out_specs=pl.BlockSpec((tm, B_WIDTH), lambda i: (i, 0)),
        out_shape=jax.ShapeDtypeStruct((m, B_WIDTH), BF16),
        compiler_params=_params(("parallel",), 48),
        name="sgu",
    )(proj, proj, spatial, spatial_bias.T, b_norm)


def _even_mixer(h, g, w_in, i_bias, f_bias, a_norm, b_norm, spatial, spatial_bias, w_out, batch, seq):
    d = h.shape[1]
    n_main = 4 * A_WIDTH
    w_main = jnp.concatenate([w_in[:, :n_main], w_in[:, n_main + 2 * A_HEADS:]], axis=1).astype(BF16)
    w_gate = jnp.zeros((d, LANES), F32).at[:, :2 * A_HEADS].set(w_in[:, n_main:n_main + 2 * A_HEADS]).astype(BF16)
    proj = _norm_matmul(h, g, w_main)
    gates = _norm_matmul(h, g, w_gate, tn=LANES)
    h_a = _mlstm(proj, gates, i_bias, f_bias, a_norm, batch, seq)
    blk = n_main // B_WIDTH
    h_b = _sgu(proj, spatial, spatial_bias, b_norm, seq, blk, blk + 1)
    w_out = w_out.astype(BF16)
    return _matmul_res([h_a, h_b], [w_out[:A_WIDTH], w_out[A_WIDTH:]], h)


def _rope_tables(seq):
    pos = jnp.arange(seq, dtype=jnp.int32).astype(F32)

    def angles(d):
        inv = jnp.power(jnp.float32(ROPE_THETA), -jnp.arange(0, d, 2, dtype=F32) / d)
        ang = pos[:, None] * inv[None, :]
        return jnp.cos(ang), jnp.sin(ang)

    cos, sin = angles(C_HEAD_DIM)
    cq = jnp.concatenate([cos, cos], axis=1)
    sq = jnp.concatenate([-sin, sin], axis=1)
    cos, sin = angles(IDX_ROPE_DIM)
    half = IDX_ROPE_DIM // 2
    rest = IDX_DIM - IDX_ROPE_DIM
    zeros_h, zeros_r = jnp.zeros((seq, half), F32), jnp.zeros((seq, rest), F32)
    ci = jnp.concatenate([cos, cos, jnp.ones((seq, rest), F32)], axis=1)
    s_from_left = jnp.concatenate([zeros_h, sin, zeros_r], axis=1)
    s_from_right = jnp.concatenate([-sin, zeros_h, zeros_r], axis=1)
    rep = LANES // IDX_DIM
    return cq, sq, jnp.tile(ci, (1, rep)), jnp.tile(s_from_left, (1, rep)), jnp.tile(s_from_right, (1, rep))


def _odd_prep_kernel(main_ref, misc_ref, cq_ref, sq_ref, ci_ref, sl_ref, sr_ref, qn_ref, kn_ref, lg_ref, lb_ref,
                     q_out, k_out, v_out, iq_out, ik_out, iw_out):
    dh = C_HEAD_DIM
    cq, sq = cq_ref[...], sq_ref[...]
    ci, sl, sr = ci_ref[...], sl_ref[...], sr_ref[...]
    half = IDX_ROPE_DIM // 2

    def rope_full(x):
        return x * cq + pltpu.roll(x, dh // 2, 1) * sq

    def rope_idx(x):
        return x * ci + pltpu.roll(x, half, 1) * sl + pltpu.roll(x, LANES - half, 1) * sr

    qn = qn_ref[...] * (dh ** -0.5 * LOG2_E)
    for h in range(C_HEADS):
        hs = slice(h * dh, (h + 1) * dh)
        q_out[h] = rope_full(_rms(main_ref[:, hs], qn)).astype(q_out.dtype)
    k0 = C_HEADS * dh
    for h in range(C_KV_HEADS):
        hs = slice(h * dh, (h + 1) * dh)
        src = slice(k0 + h * dh, k0 + (h + 1) * dh)
        k_out[:, hs] = rope_full(_rms(main_ref[:, src], kn_ref[...])).astype(k_out.dtype)
    v0 = k0 + C_KV_HEADS * dh
    v_out[...] = main_ref[:, v0:v0 + C_KV_HEADS * dh].astype(v_out.dtype)
    i0 = v0 + C_KV_HEADS * dh
    per_tile = LANES // IDX_DIM
    for t in range(IDX_HEADS // per_tile):
        y = rope_idx(main_ref[:, i0 + t * LANES:i0 + (t + 1) * LANES]).T
        for j in range(per_tile):
            iq_out[t * per_tile + j] = y[j * IDX_DIM:(j + 1) * IDX_DIM, :].astype(iq_out.dtype)

    misc = misc_ref[...]
    lane = lax.broadcasted_iota(jnp.int32, misc.shape, 1)
    is_key = lane < IDX_DIM
    mu = jnp.sum(jnp.where(is_key, misc, 0.0), axis=1, keepdims=True) / IDX_DIM
    cen = jnp.where(is_key, misc - mu, 0.0)
    var = jnp.sum(cen * cen, axis=1, keepdims=True) / IDX_DIM
    ikn = cen * lax.rsqrt(var + NORM_EPS) * lg_ref[...] + lb_ref[...]
    ik_out[...] = rope_idx(ikn)[:, :IDX_DIM].astype(ik_out.dtype)
    iw_out[...] = misc.T[IDX_DIM:IDX_DIM + IDX_HEADS, :] * (IDX_HEADS ** -0.5 * IDX_DIM ** -0.5)


def _odd_prep(main, misc, q_norm, k_norm, ln_g, ln_b, seq):
    m = main.shape[0]
    tm = _tile(seq, 256)
    ns = seq // tm
    tables = _rope_tables(seq)
    pad = lambda v: jnp.zeros((1, LANES), F32).at[0, :IDX_DIM].set(v)
    tab = pl.BlockSpec((tm, LANES), lambda i: (i % ns, 0))
    vec = pl.BlockSpec((1, LANES), lambda i: (0, 0))
    qw, kw = C_HEADS * C_HEAD_DIM, C_KV_HEADS * C_HEAD_DIM
    return pl.pallas_call(
        _odd_prep_kernel,
        grid=(m // tm,),
        in_specs=[pl.BlockSpec((tm, main.shape[1]), lambda i: (i, 0)),
                  pl.BlockSpec((tm, LANES), lambda i: (i, 0)),
                  tab, tab, tab, tab, tab, vec, vec, vec, vec],
        out_specs=[pl.BlockSpec((C_HEADS, tm, C_HEAD_DIM), lambda i: (0, i, 0)),
                   pl.BlockSpec((tm, kw), lambda i: (i, 0)),
                   pl.BlockSpec((tm, kw), lambda i: (i, 0)),
                   pl.BlockSpec((IDX_HEADS, IDX_DIM, tm), lambda i: (0, 0, i)),
                   pl.BlockSpec((tm, IDX_DIM), lambda i: (i, 0)),
                   pl.BlockSpec((IDX_HEADS, tm), lambda i: (0, i))],
        out_shape=[jax.ShapeDtypeStruct((C_HEADS, m, C_HEAD_DIM), BF16),
                   jax.ShapeDtypeStruct((m, kw), BF16),
                   jax.ShapeDtypeStruct((m, kw), BF16),
                   jax.ShapeDtypeStruct((IDX_HEADS, IDX_DIM, m), BF16),
                   jax.ShapeDtypeStruct((m, IDX_DIM), BF16),
                   jax.ShapeDtypeStruct((IDX_HEADS, m), F32)],
        compiler_params=_params(("parallel",), 48),
        name="odd_prep",
    )(main, misc, *tables, q_norm.reshape(1, LANES), k_norm.reshape(1, LANES), pad(ln_g), pad(ln_b))


def _indexer_kernel(iq_ref, iw_ref, ik_ref, o_ref, key_ref, *, tq, tk, nkb, topk, pos_bits):
    qi = pl.program_id(1)
    n_causal = ((qi + 1) * tq + tk - 1) // tk
    q_pos = qi * tq + lax.broadcasted_iota(jnp.int32, (1, tq), 1)
    sub = lax.broadcasted_iota(jnp.int32, (SUBLANES, tq), 0)
    iw = iw_ref[...]

    def score_body(kb, carry):
        k0 = pl.multiple_of(kb * tk, tk)
        for c in range(tk // IDX_KEYS):
            ikc = ik_ref[pl.ds(k0 + c * IDX_KEYS, IDX_KEYS), :]
            acc = jnp.zeros((IDX_KEYS, tq), F32)
            for h in range(IDX_HEADS):
                acc = acc + iw[h:h + 1, :] * jnp.maximum(_dot(ikc, iq_ref[h]), 0.0)
            bits = lax.bitcast_convert_type(acc, jnp.int32)
            key = bits ^ ((bits >> 31) & jnp.int32(0x7FFFFFFF))
            k_pos = k0 + c * IDX_KEYS + lax.broadcasted_iota(jnp.int32, (IDX_KEYS, 1), 0)
            key_ref[kb, c * IDX_KEYS:(c + 1) * IDX_KEYS, :] = jnp.where(k_pos <= q_pos, key, INT_MIN)
        return carry

    lax.fori_loop(0, n_causal, score_body, 0)

    def count(pred):
        def body(kb, cs):
            cs = list(cs)
            for g in range(tk // SUBLANES):
                key = key_ref[kb, g * SUBLANES:(g + 1) * SUBLANES, :]
                hit = pred(key, kb * tk + g * SUBLANES + sub).astype(jnp.int32)
                cs[g % len(cs)] = cs[g % len(cs)] + hit
            return tuple(cs)
        zero = jnp.zeros((SUBLANES, tq), jnp.int32)
        cs = lax.fori_loop(0, n_causal, body, (zero,) * IDX_ACCS)
        total = functools.reduce(jnp.add, cs)
        return jnp.broadcast_to(jnp.sum(total, axis=0, keepdims=True), (SUBLANES, tq))

    def bisect(it, thr):
        cand = thr + jnp.left_shift(jnp.int32(1), 31 - it)
        return jnp.where(count(lambda key, pos: key >= cand) >= topk, cand, thr)

    thr = lax.fori_loop(0, 32, bisect, jnp.full((SUBLANES, tq), INT_MIN, jnp.int32))

    def tie_limit():
        keep = topk - count(lambda key, pos: key > thr)

        def step(it, below):
            cand = below + jnp.left_shift(jnp.int32(1), pos_bits - 1 - it)
            n_le = count(lambda key, pos: (key == thr) & (pos <= cand))
            return jnp.where(n_le < keep, cand, below)

        below = lax.fori_loop(0, pos_bits, step, jnp.full((SUBLANES, tq), -1, jnp.int32))
        return below + 1

    n_ge = count(lambda key, pos: key >= thr)
    has_excess_ties = jnp.max(n_ge.astype(F32)) > topk
    lim = lax.cond(has_excess_ties, tie_limit, lambda: jnp.full((SUBLANES, tq), 2 ** pos_bits, jnp.int32))
    thr_row, lim_row = thr[0:1, :], lim[0:1, :]

    def write_body(kb, carry):
        for j in range(tk // LANES):
            key = key_ref[kb, j * LANES:(j + 1) * LANES, :]
            pos = kb * tk + j * LANES + lax.broadcasted_iota(jnp.int32, (LANES, 1), 0)
            sel = ((key > thr_row) | ((key == thr_row) & (pos <= lim_row))) & (pos <= q_pos)
            bias = jnp.where(sel, 0.0, MASK_NEG)
            o_ref[kb, :, j * LANES:(j + 1) * LANES] = bias.T.astype(o_ref.dtype)
        return carry

    lax.fori_loop(0, n_causal, write_body, 0)

    def fill_body(kb, carry):
        o_ref[kb] = jnp.full((tq, tk), MASK_NEG, o_ref.dtype)
        return carry

    lax.fori_loop(n_causal, nkb, fill_body, 0)


def _indexer(iq_t, iw_t, ik, batch, seq, tq, tk):
    nkb = seq // tk
    nq = seq // tq
    topk = min(TOPK_MAX, seq // 4)
    return pl.pallas_call(
        functools.partial(_indexer_kernel, tq=tq, tk=tk, nkb=nkb, topk=topk,
                          pos_bits=max(1, (seq - 1).bit_length())),
        grid=(batch, nq),
        in_specs=[pl.BlockSpec((IDX_HEADS, IDX_DIM, tq), lambda b, i: (0, 0, b * nq + i)),
                  pl.BlockSpec((IDX_HEADS, tq), lambda b, i: (0, b * nq + i)),
                  pl.BlockSpec((seq, IDX_DIM), lambda b, i: (b, 0))],
        out_specs=pl.BlockSpec((None, nkb, tq, tk), lambda b, i: (b, 0, i, 0)),
        out_shape=jax.ShapeDtypeStruct((batch, nkb, seq, tk), BF16),
        scratch_shapes=[pltpu.VMEM((nkb, tk, tq), jnp.int32)],
        compiler_params=_params(("parallel", "parallel"), 56),
        name="indexer",
    )(iq_t, iw_t, ik)


def _attn_kernel(q_ref, k_ref, v_ref, b_ref, o_ref, acc_ref, m_ref, l_ref, s_ref, p_ref, bias_ref, *, tq, tk):
    dh = C_HEAD_DIM
    group = C_HEADS // C_KV_HEADS
    qi, kb = pl.program_id(1), pl.program_id(2)
    last_kb = (qi * tq + tq - 1) // tk

    @pl.when(kb == 0)
    def _():
        m_ref[...] = jnp.full_like(m_ref, MASK_NEG)
        l_ref[...] = jnp.zeros_like(l_ref)
        acc_ref[...] = jnp.zeros_like(acc_ref)

    @pl.when(kb <= last_kb)
    def _():
        rows = group * tq
        bias_ref[...] = b_ref[...].astype(F32)

        def logits(g):
            qg = q_ref[g * group:(g + 1) * group].reshape(rows, dh)
            s_ref[g % 2] = _dot_nt(qg, k_ref[:, g * dh:(g + 1) * dh])

        logits(0)
        for g in range(C_KV_HEADS):
            if g + 1 < C_KV_HEADS:
                logits(g + 1)
            for c in range(rows // ATTN_ROWS):
                rs = slice(c * ATTN_ROWS, (c + 1) * ATTN_ROWS)
                brs = slice((c * ATTN_ROWS) % tq, (c * ATTN_ROWS) % tq + ATTN_ROWS)
                tiles = [s_ref[g % 2, rs, t * LANES:(t + 1) * LANES] + bias_ref[brs, t * LANES:(t + 1) * LANES]
                         for t in range(tk // LANES)]
                tile_max = functools.reduce(jnp.maximum, tiles)
                m_old = m_ref[g, rs, :]
                m_new = jnp.maximum(m_old, jnp.max(tile_max, axis=1, keepdims=True))
                alpha = jnp.exp2(m_old - m_new)
                ps = [jnp.exp2(t - m_new) for t in tiles]
                l_ref[g, rs, :] = alpha * l_ref[g, rs, :] + jnp.sum(functools.reduce(jnp.add, ps), axis=1,
                                                                  keepdims=True)
                m_ref[g, rs, :] = m_new
                acc_ref[g, rs, :] = alpha * acc_ref[g, rs, :]
                for t, p in enumerate(ps):
                    p_ref[g % 2, rs, t * LANES:(t + 1) * LANES] = p.astype(BF16)
            acc_ref[g] += _dot(p_ref[g % 2], v_ref[:, g * dh:(g + 1) * dh])

    @pl.when(kb == last_kb)
    def _():
        for g in range(C_KV_HEADS):
            out = acc_ref[g] / l_ref[g]
            for j in range(group):
                h = g * group + j
                o_ref[:, h * dh:(h + 1) * dh] = out[j * tq:(j + 1) * tq].astype(o_ref.dtype)


def _attention(q, k, v, mask, batch, seq, tq, tk):
    m = k.shape[0]
    nq, nkb = seq // tq, seq // tk
    qw, kw = C_HEADS * C_HEAD_DIM, C_KV_HEADS * C_HEAD_DIM
    rows = (C_HEADS // C_KV_HEADS) * tq
    clamp = lambda i, j: jnp.minimum(j, (i * tq + tq - 1) // tk)
    return pl.pallas_call(
        functools.partial(_attn_kernel, tq=tq, tk=tk),
        grid=(batch, nq, nkb),
        in_specs=[pl.BlockSpec((C_HEADS, tq, C_HEAD_DIM), lambda b, i, j: (0, b * nq + i, 0)),
                  pl.BlockSpec((tk, kw), lambda b, i, j: (b * nkb + clamp(i, j), 0)),
                  pl.BlockSpec((tk, kw), lambda b, i, j: (b * nkb + clamp(i, j), 0)),
                  pl.BlockSpec((None, None, tq, tk), lambda b, i, j: (b, clamp(i, j), i, 0))],
        out_specs=pl.BlockSpec((tq, qw), lambda b, i, j: (b * nq + i, 0)),
        out_shape=jax.ShapeDtypeStruct((m, qw), BF16),
        scratch_shapes=[pltpu.VMEM((C_KV_HEADS, rows, C_HEAD_DIM), F32),
                        pltpu.VMEM((C_KV_HEADS, rows, LANES), F32),
                        pltpu.VMEM((C_KV_HEADS, rows, LANES), F32),
                        pltpu.VMEM((2, rows, tk), F32),
                        pltpu.VMEM((2, rows, tk), BF16),
                        pltpu.VMEM((tq, tk), F32)],
        compiler_params=_params(("parallel", "parallel", "arbitrary"), 48),
        name="attention",
    )(q, k, v, mask)


def _odd_mixer(h, g, w_in, q_norm, k_norm, ln_g, ln_b, w_out, batch, seq):
    d = h.shape[1]
    n_main = (C_HEADS + 2 * C_KV_HEADS) * C_HEAD_DIM + IDX_HEADS * IDX_DIM
    n_misc = IDX_DIM + IDX_HEADS
    w_main = w_in[:, :n_main].astype(BF16)
    w_misc = jnp.zeros((d, LANES), F32).at[:, :n_misc].set(w_in[:, n_main:]).astype(BF16)
    main = _norm_matmul(h, g, w_main)
    misc = _norm_matmul(h, g, w_misc, tn=LANES)
    q, k, v, iq, ik, iw = _odd_prep(main, misc, q_norm, k_norm, ln_g, ln_b, seq)
    tq, tk = _tile(seq, 256), _tile(seq, 1024)
    mask = _indexer(iq, iw, ik, batch, seq, tq, tk)
    att = _attention(q, k, v, mask, batch, seq, tq, tk)
    return _matmul_res([att], [w_out.astype(BF16)], h)


def kernel(x, mix_norm, even_w_in, even_i_bias, even_f_bias, even_a_norm, even_b_norm, even_spatial,
           even_spatial_bias, even_w_out, odd_w_in, odd_q_norm, odd_k_norm, odd_idx_ln_g, odd_idx_ln_b,
           odd_w_out, ffn_norm, ffn_w_up, ffn_conv_w, ffn_conv_b, ffn_w_down):
    batch, seq, d = x.shape
    depth = mix_norm.shape[0]
    h = x.reshape(batch * seq, d)
    for layer in range(depth):
        j = layer // 2
        if layer % 2 == 0:
            h = _even_mixer(h, mix_norm[layer], even_w_in[j], even_i_bias[j], even_f_bias[j], even_a_norm[j],
                            even_b_norm[j], even_spatial[j], even_spatial_bias[j], even_w_out[j], batch, seq)
        else:
            h = _odd_mixer(h, mix_norm[layer], odd_w_in[j], odd_q_norm[j], odd_k_norm[j], odd_idx_ln_g[j],
                           odd_idx_ln_b[j], odd_w_out[j], batch, seq)
        h = _conv_ffn(h, ffn_norm[layer], ffn_w_up[layer], ffn_conv_w[layer], ffn_conv_b[layer],
                      ffn_w_down[layer], seq)
    return h.reshape(batch, seq, d)
```

```python
import functools

import jax
import jax.numpy as jnp
from jax import lax
from jax.experimental import pallas as pl
from jax.experimental.pallas import tpu as pltpu

A_HEADS = 4
A_HEAD_DIM = 256
B_GROUPS = 4
B_GROUP_DIM = 256
B_CHUNK = 128
C_HEADS = 16
C_KV_HEADS = 4
C_HEAD_DIM = 128
IDX_HEADS = 16
IDX_DIM = 64
IDX_ROPE_DIM = 32
TOPK_MAX = 256
ROPE_THETA = 10000.0
CONV_WIDTH = 3
NORM_EPS = 1e-6

A_WIDTH = A_HEADS * A_HEAD_DIM
B_WIDTH = B_GROUPS * B_GROUP_DIM

LANES = 128
HALO = 16
MLSTM_CHUNK = 256
ATTN_ROWS = 32
FFN_ROW_PARTS = 1
FFN_COL_PARTS = 2
SUBLANES = 8
PACKED_ROWS = 16
HALF_RANGE = 32768
IDX_KEYS = 256
IDX_ACCS = 4
MASK_NEG = -1e30
LOG2_E = 1.4426950408889634
INT_MIN = -(2 ** 31)

F32 = jnp.float32
BF16 = jnp.bfloat16


def _params(dims, vmem_mb):
    return pltpu.CompilerParams(dimension_semantics=dims, vmem_limit_bytes=vmem_mb * 1024 * 1024)


def _rms(x, g):
    return x * lax.rsqrt(jnp.mean(x * x, axis=-1, keepdims=True) + NORM_EPS) * g


def _log_sigmoid(x):
    return jnp.minimum(x, 0.0) - jnp.log1p(jnp.exp(-jnp.abs(x)))


def _dot(a, b):
    return jnp.dot(a, b, preferred_element_type=F32)


def _dot_nt(a, b):
    return lax.dot_general(a, b, (((1,), (1,)), ((), ())), preferred_element_type=F32)


def _dot_tn(a, b):
    return lax.dot_general(a, b, (((0,), (0,)), ((), ())), preferred_element_type=F32)


def _tile(n, pref):
    t = min(n, pref)
    assert n % t == 0, (n, pref)
    return t


def _norm_matmul_kernel(x_ref, g_ref, w_ref, o_ref, xn_ref):
    @pl.when(pl.program_id(1) == 0)
    def _():
        xn_ref[...] = _rms(x_ref[...], g_ref[...]).astype(BF16)

    o_ref[...] = _dot(xn_ref[...], w_ref[...]).astype(o_ref.dtype)


def _norm_matmul(x, g, w, out_dtype=F32, tm=1024, tn=1024):
    m, d = x.shape
    n = w.shape[1]
    tm, tn = _tile(m, tm), _tile(n, tn)
    return pl.pallas_call(
        _norm_matmul_kernel,
        grid=(m // tm, n // tn),
        in_specs=[pl.BlockSpec((tm, d), lambda i, j: (i, 0)),
                  pl.BlockSpec((1, d), lambda i, j: (0, 0)),
                  pl.BlockSpec((d, tn), lambda i, j: (0, j))],
        out_specs=pl.BlockSpec((tm, tn), lambda i, j: (i, j)),
        out_shape=jax.ShapeDtypeStruct((m, n), out_dtype),
        scratch_shapes=[pltpu.VMEM((tm, d), BF16)],
        compiler_params=_params(("parallel", "arbitrary"), 48),
        name="norm_matmul",
    )(x, g.reshape(1, d), w)


def _matmul_res_kernel(*refs, n_lhs):
    lhs, ws = refs[:n_lhs], refs[n_lhs:2 * n_lhs]
    res_ref, o_ref = refs[2 * n_lhs], refs[2 * n_lhs + 1]
    acc = res_ref[...]
    for a_ref, w_ref in zip(lhs, ws):
        acc = acc + _dot(a_ref[...], w_ref[...])
    o_ref[...] = acc


def _matmul_res(lhs_list, w_list, res, tm=1024, tn=1024):
    m, n = res.shape
    tm, tn = _tile(m, tm), _tile(n, tn)
    in_specs = ([pl.BlockSpec((tm, a.shape[1]), lambda i, j: (i, 0)) for a in lhs_list]
                + [pl.BlockSpec((w.shape[0], tn), lambda i, j: (0, j)) for w in w_list]
                + [pl.BlockSpec((tm, tn), lambda i, j: (i, j))])
    return pl.pallas_call(
        functools.partial(_matmul_res_kernel, n_lhs=len(lhs_list)),
        grid=(m // tm, n // tn),
        in_specs=in_specs,
        out_specs=pl.BlockSpec((tm, tn), lambda i, j: (i, j)),
        out_shape=jax.ShapeDtypeStruct((m, n), F32),
        compiler_params=_params(("parallel", "parallel"), 56),
        name="matmul_res",
    )(*lhs_list, *w_list, res)


def _ffn_up_kernel(x_ref, halo_ref, g_ref, wa_ref, wb_ref, cwa_ref, cwb_ref, o_ref, xn_ref, sa_ref, sb_ref,
                   *, tm, seq):
    i = pl.program_id(0)

    @pl.when(pl.program_id(1) == 0)
    def _():
        g = g_ref[...]
        starts_sequence = (i * tm) % seq == 0
        halo = jnp.where(starts_sequence, 0.0, _rms(halo_ref[...], g))
        xn_ref[0:HALO, :] = halo.astype(BF16)
        xn_ref[HALO:, :] = _rms(x_ref[...], g).astype(BF16)

    tn = o_ref.shape[1]
    rows, cols = tm // FFN_ROW_PARTS, tn // FFN_COL_PARTS

    def conv(s_ref, cw_ref, rp, cs):
        out = cw_ref[CONV_WIDTH:CONV_WIDTH + 1, cs]
        for tap in range(CONV_WIDTH):
            lo = HALO - (CONV_WIDTH - 1) + tap
            out = out + cw_ref[tap:tap + 1, cs] * s_ref[rp, lo:lo + rows, cs]
        return out

    for rp in range(FFN_ROW_PARTS):
        xn = xn_ref[rp * rows:(rp + 1) * rows + HALO, :]
        for cp in range(FFN_COL_PARTS):
            cs = slice(cp * cols, (cp + 1) * cols)
            sa_ref[rp, :, cs] = _dot(xn, wa_ref[:, cs])
            sb_ref[rp, :, cs] = _dot(xn, wb_ref[:, cs])
            a = conv(sa_ref, cwa_ref, rp, cs)
            b = conv(sb_ref, cwb_ref, rp, cs)
            o_ref[rp * rows:(rp + 1) * rows, cs] = (a * jax.nn.sigmoid(a) * b).astype(o_ref.dtype)


def _ffn_up(x, g, w_up, conv_w, conv_b, seq, tm=1024, tn=512):
    m, d = x.shape
    f = w_up.shape[1] // 2
    tm, tn = _tile(seq, tm), _tile(f, tn)
    nj = f // tn
    conv_wb = jnp.concatenate([conv_w, conv_b.reshape(1, 2 * f)], axis=0)
    return pl.pallas_call(
        functools.partial(_ffn_up_kernel, tm=tm, seq=seq),
        grid=(m // tm, nj),
        in_specs=[pl.BlockSpec((tm, d), lambda i, j: (i, 0)),
                  pl.BlockSpec((HALO, d), lambda i, j: (jnp.maximum(i * (tm // HALO) - 1, 0), 0)),
                  pl.BlockSpec((1, d), lambda i, j: (0, 0)),
                  pl.BlockSpec((d, tn), lambda i, j: (0, j)),
                  pl.BlockSpec((d, tn), lambda i, j: (0, j + nj)),
                  pl.BlockSpec((CONV_WIDTH + 1, tn), lambda i, j: (0, j)),
                  pl.BlockSpec((CONV_WIDTH + 1, tn), lambda i, j: (0, j + nj))],
        out_specs=pl.BlockSpec((tm, tn), lambda i, j: (i, j)),
        out_shape=jax.ShapeDtypeStruct((m, f), BF16),
        scratch_shapes=[pltpu.VMEM((tm + HALO, d), BF16),
                        pltpu.VMEM((FFN_ROW_PARTS, tm // FFN_ROW_PARTS + HALO, tn), F32),
                        pltpu.VMEM((FFN_ROW_PARTS, tm // FFN_ROW_PARTS + HALO, tn), F32)],
        compiler_params=_params(("parallel", "arbitrary"), 48),
        name="ffn_up",
    )(x, x, g.reshape(1, d), w_up, w_up, conv_wb, conv_wb)


def _conv_ffn(h, g, w_up, conv_w, conv_b, w_down, seq):
    gated = _ffn_up(h, g, w_up.astype(BF16), conv_w, conv_b, seq)
    return _matmul_res([gated], [w_down.astype(BF16)], h, tn=512)


def _mlstm_kernel(q_ref, k_ref, v_ref, o_ref, grow_ref, gcol_ref, bcol_ref, brow_ref, an_ref, out_ref,
                  c_ref, n_ref, m_ref, *, L):
    H, dh = A_HEADS, A_HEAD_DIM

    @pl.when(pl.program_id(1) == 0)
    def _():
        c_ref[...] = jnp.zeros_like(c_ref)
        n_ref[...] = jnp.zeros_like(n_ref)
        m_ref[...] = jnp.zeros_like(m_ref)

    row = lax.broadcasted_iota(jnp.int32, (L, L), 0)
    col = lax.broadcasted_iota(jnp.int32, (L, L), 1)
    lower = col <= row
    upper = row <= col
    grow = grow_ref[...] + bcol_ref[...]
    gcol = gcol_ref[...] + brow_ref[...]

    for h in range(H):
        hs = slice(h * dh, (h + 1) * dh)
        i_row, lf_row = grow[h:h + 1, :], _log_sigmoid(grow[H + h:H + h + 1, :])
        i_col, lf_col = gcol[:, h:h + 1], _log_sigmoid(gcol[:, H + h:H + h + 1])
        b_col = jnp.sum(jnp.where(lower, lf_row, 0.0), axis=1, keepdims=True)
        b_row = jnp.sum(jnp.where(upper, lf_col, 0.0), axis=0, keepdims=True)
        a_row = i_row - b_row
        a_col = i_col - b_col
        m_prev = m_ref[h][:, 0:1]
        m_col = jnp.maximum(m_prev, jnp.max(jnp.where(lower, a_row, -jnp.inf), axis=1, keepdims=True))
        e = jnp.where(lower, jnp.exp(a_row - m_col), 0.0)

        q = q_ref[:, hs] * (dh ** -0.5)
        kf = k_ref[:, hs]
        qb, kb, vb = q.astype(BF16), kf.astype(BF16), v_ref[:, hs].astype(BF16)
        s = _dot_nt(qb, kb) * e
        w_inter = jnp.exp(m_prev - m_col)
        num = _dot(s.astype(BF16), vb) + w_inter * _dot(qb, c_ref[h].astype(BF16))
        den = jnp.sum(s, axis=1, keepdims=True) + w_inter * jnp.sum(q * n_ref[h], axis=1, keepdims=True)
        hh = num / jnp.maximum(jnp.abs(den), jnp.exp(-(b_col + m_col)))

        m_last = m_col[L - 1:L, :]
        decay = jnp.exp(m_prev - m_last)
        wk = jnp.exp(a_col - m_last) * kf
        c_ref[h] = decay * c_ref[h] + _dot_tn(wk.astype(BF16), vb)
        n_ref[h] = decay * n_ref[h] + jnp.sum(wk, axis=0, keepdims=True)
        m_ref[h] = jnp.broadcast_to(b_col[L - 1:L, :] + m_last, (1, LANES))

        y = _rms(hh, an_ref[h:h + 1, :]) * jax.nn.sigmoid(o_ref[:, hs])
        out_ref[:, hs] = y.astype(out_ref.dtype)


def _mlstm(proj, gates, i_bias, f_bias, a_norm, batch, seq):
    m = proj.shape[0]
    L = _tile(seq, MLSTM_CHUNK)
    nc = seq // L
    H = A_HEADS
    nw = A_WIDTH // A_WIDTH
    del nw
    bias = jnp.concatenate([i_bias, f_bias]).astype(F32)
    bias_col = bias.reshape(2 * H, 1)
    bias_row = jnp.zeros((1, LANES), F32).at[0, :2 * H].set(bias)
    gates_row = gates[:, :2 * H].reshape(batch, seq, 2 * H).transpose(0, 2, 1)
    qkvo = lambda c: pl.BlockSpec((L, A_WIDTH), lambda b, t: (b * nc + t, c))
    return pl.pallas_call(
        functools.partial(_mlstm_kernel, L=L),
        grid=(batch, nc),
        in_specs=[qkvo(0), qkvo(1), qkvo(2), qkvo(3),
                  pl.BlockSpec((None, 2 * H, L), lambda b, t: (b, 0, t)),
                  pl.BlockSpec((L, LANES), lambda b, t: (b * nc + t, 0)),
                  pl.BlockSpec((2 * H, 1), lambda b, t: (0, 0)),
                  pl.BlockSpec((1, LANES), lambda b, t: (0, 0)),
                  pl.BlockSpec((H, A_HEAD_DIM), lambda b, t: (0, 0))],
        out_specs=pl.BlockSpec((L, A_WIDTH), lambda b, t: (b * nc + t, 0)),
        out_shape=jax.ShapeDtypeStruct((m, A_WIDTH), BF16),
        scratch_shapes=[pltpu.VMEM((H, A_HEAD_DIM, A_HEAD_DIM), F32),
                        pltpu.VMEM((H, 1, A_HEAD_DIM), F32),
                        pltpu.VMEM((H, 1, LANES), F32)],
        compiler_params=_params(("parallel", "arbitrary"), 48),
        name="mlstm",
    )(proj, proj, proj, proj, gates_row, gates, bias_col, bias_row, a_norm)


def _sgu_kernel(u_ref, z_ref, sp_ref, sb_ref, bn_ref, o_ref, *, chunks):
    T, gd = B_CHUNK, B_GROUP_DIM
    row = lax.broadcasted_iota(jnp.int32, (T, T), 0)
    col = lax.broadcasted_iota(jnp.int32, (T, T), 1)
    lower = col <= row
    for g in range(B_GROUPS):
        gs = slice(g * gd, (g + 1) * gd)
        w = jnp.where(lower, sp_ref[g], 0.0).astype(BF16)
        bias = sb_ref[:, g:g + 1]
        bn = bn_ref[g:g + 1, :]
        for c in range(chunks):
            rs = slice(c * T, (c + 1) * T)
            zn = _rms(jax.nn.gelu(z_ref[rs, gs]), bn)
            zmix = _dot(w, zn.astype(BF16)) + bias
            o_ref[rs, gs] = (jax.nn.gelu(u_ref[rs, gs]) * zmix).astype(o_ref.dtype)


def _sgu(proj, spatial, spatial_bias, b_norm, seq, u_block, z_block):
    m = proj.shape[0]
    tm = _tile(seq, 512)
    chunks = tm // B_CHUNK
    return pl.pallas_call(
        functools.partial(_sgu_kernel, chunks=chunks),
        grid=(m // tm,),
        in_specs=[pl.BlockSpec((tm, B_WIDTH), lambda i: (i, u_block)),
                  pl.BlockSpec((tm, B_WIDTH), lambda i: (i, z_block)),
                  pl.BlockSpec((B_GROUPS, B_CHUNK, B_CHUNK), lambda i: (0, 0, 0)),
                  pl.BlockSpec((B_CHUNK, B_GROUPS), lambda i: (0, 0)),
                  pl.BlockSpec((B_GROUPS, B_GROUP_DIM), lambda i: (0, 0))],
        out_specs=pl.BlockSpec((tm, B_WIDTH), lambda i: (i, 0)),
        out_shape=jax.ShapeDtypeStruct((m, B_WIDTH), BF16),
        compiler_params=_params(("parallel",), 48),
        name="sgu",
    )(proj, proj, spatial, spatial_bias.T, b_norm)


def _even_mixer(h, g, w_in, i_bias, f_bias, a_norm, b_norm, spatial, spatial_bias, w_out, batch, seq):
    d = h.shape[1]
    n_main = 4 * A_WIDTH
    w_main = jnp.concatenate([w_in[:, :n_main], w_in[:, n_main + 2 * A_HEADS:]], axis=1).astype(BF16)
    w_gate = jnp.zeros((d, LANES), F32).at[:, :2 * A_HEADS].set(w_in[:, n_main:n_main + 2 * A_HEADS]).astype(BF16)
    proj = _norm_matmul(h, g, w_main)
    gates = _norm_matmul(h, g, w_gate, tn=LANES)
    h_a = _mlstm(proj, gates, i_bias, f_bias, a_norm, batch, seq)
    blk = n_main // B_WIDTH
    h_b = _sgu(proj, spatial, spatial_bias, b_norm, seq, blk, blk + 1)
    w_out = w_out.astype(BF16)
    return _matmul_res([h_a, h_b], [w_out[:A_WIDTH], w_out[A_WIDTH:]], h)


def _rope_tables(seq):
    pos = jnp.arange(seq, dtype=jnp.int32).astype(F32)

    def angles(d):
        inv = jnp.power(jnp.float32(ROPE_THETA), -jnp.arange(0, d, 2, dtype=F32) / d)
        ang = pos[:, None] * inv[None, :]
        return jnp.cos(ang), jnp.sin(ang)

    cos, sin = angles(C_HEAD_DIM)
    cq = jnp.concatenate([cos, cos], axis=1)
    sq = jnp.concatenate([-sin, sin], axis=1)
    cos, sin = angles(IDX_ROPE_DIM)
    half = IDX_ROPE_DIM // 2
    rest = IDX_DIM - IDX_ROPE_DIM
    zeros_h, zeros_r = jnp.zeros((seq, half), F32), jnp.zeros((seq, rest), F32)
    ci = jnp.concatenate([cos, cos, jnp.ones((seq, rest), F32)], axis=1)
    s_from_left = jnp.concatenate([zeros_h, sin, zeros_r], axis=1)
    s_from_right = jnp.concatenate([-sin, zeros_h, zeros_r], axis=1)
    rep = LANES // IDX_DIM
    return cq, sq, jnp.tile(ci, (1, rep)), jnp.tile(s_from_left, (1, rep)), jnp.tile(s_from_right, (1, rep))


def _odd_prep_kernel(main_ref, misc_ref, cq_ref, sq_ref, ci_ref, sl_ref, sr_ref, qn_ref, kn_ref, lg_ref, lb_ref,
                     q_out, k_out, v_out, iq_out, ik_out, iw_out):
    dh = C_HEAD_DIM
    cq, sq = cq_ref[...], sq_ref[...]
    ci, sl, sr = ci_ref[...], sl_ref[...], sr_ref[...]
    half = IDX_ROPE_DIM // 2

    def rope_full(x):
        return x * cq + pltpu.roll(x, dh // 2, 1) * sq

    def rope_idx(x):
        return x * ci + pltpu.roll(x, half, 1) * sl + pltpu.roll(x, LANES - half, 1) * sr

    qn = qn_ref[...] * (dh ** -0.5 * LOG2_E)
    for h in range(C_HEADS):
        hs = slice(h * dh, (h + 1) * dh)
        q_out[h] = rope_full(_rms(main_ref[:, hs], qn)).astype(q_out.dtype)
    k0 = C_HEADS * dh
    for h in range(C_KV_HEADS):
        hs = slice(h * dh, (h + 1) * dh)
        src = slice(k0 + h * dh, k0 + (h + 1) * dh)
        k_out[:, hs] = rope_full(_rms(main_ref[:, src], kn_ref[...])).astype(k_out.dtype)
    v0 = k0 + C_KV_HEADS * dh
    v_out[...] = main_ref[:, v0:v0 + C_KV_HEADS * dh].astype(v_out.dtype)
    i0 = v0 + C_KV_HEADS * dh
    per_tile = LANES // IDX_DIM
    for t in range(IDX_HEADS // per_tile):
        y = rope_idx(main_ref[:, i0 + t * LANES:i0 + (t + 1) * LANES]).T
        for j in range(per_tile):
            iq_out[t * per_tile + j] = y[j * IDX_DIM:(j + 1) * IDX_DIM, :].astype(iq_out.dtype)

    misc = misc_ref[...]
    lane = lax.broadcasted_iota(jnp.int32, misc.shape, 1)
    is_key = lane < IDX_DIM
    mu = jnp.sum(jnp.where(is_key, misc, 0.0), axis=1, keepdims=True) / IDX_DIM
    cen = jnp.where(is_key, misc - mu, 0.0)
    var = jnp.sum(cen * cen, axis=1, keepdims=True) / IDX_DIM
    ikn = cen * lax.rsqrt(var + NORM_EPS) * lg_ref[...] + lb_ref[...]
    ik_out[...] = rope_idx(ikn)[:, :IDX_DIM].astype(ik_out.dtype)
    iw_out[...] = misc.T[IDX_DIM:IDX_DIM + IDX_HEADS, :] * (IDX_HEADS ** -0.5 * IDX_DIM ** -0.5)


def _odd_prep(main, misc, q_norm, k_norm, ln_g, ln_b, seq):
    m = main.shape[0]
    tm = _tile(seq, 256)
    ns = seq // tm
    tables = _rope_tables(seq)
    pad = lambda v: jnp.zeros((1, LANES), F32).at[0, :IDX_DIM].set(v)
    tab = pl.BlockSpec((tm, LANES), lambda i: (i % ns, 0))
    vec = pl.BlockSpec((1, LANES), lambda i: (0, 0))
    qw, kw = C_HEADS * C_HEAD_DIM, C_KV_HEADS * C_HEAD_DIM
    return pl.pallas_call(
        _odd_prep_kernel,
        grid=(m // tm,),
        in_specs=[pl.BlockSpec((tm, main.shape[1]), lambda i: (i, 0)),
                  pl.BlockSpec((tm, LANES), lambda i: (i, 0)),
                  tab, tab, tab, tab, tab, vec, vec, vec, vec],
        out_specs=[pl.BlockSpec((C_HEADS, tm, C_HEAD_DIM), lambda i: (0, i, 0)),
                   pl.BlockSpec((tm, kw), lambda i: (i, 0)),
                   pl.BlockSpec((tm, kw), lambda i: (i, 0)),
                   pl.BlockSpec((IDX_HEADS, IDX_DIM, tm), lambda i: (0, 0, i)),
                   pl.BlockSpec((tm, IDX_DIM), lambda i: (i, 0)),
                   pl.BlockSpec((IDX_HEADS, tm), lambda i: (0, i))],
        out_shape=[jax.ShapeDtypeStruct((C_HEADS, m, C_HEAD_DIM), BF16),
                   jax.ShapeDtypeStruct((m, kw), BF16),
                   jax.ShapeDtypeStruct((m, kw), BF16),
                   jax.ShapeDtypeStruct((IDX_HEADS, IDX_DIM, m), BF16),
                   jax.ShapeDtypeStruct((m, IDX_DIM), BF16),
                   jax.ShapeDtypeStruct((IDX_HEADS, m), F32)],
        compiler_params=_params(("parallel",), 48),
        name="odd_prep",
    )(main, misc, *tables, q_norm.reshape(1, LANES), k_norm.reshape(1, LANES), pad(ln_g), pad(ln_b))


def _indexer_kernel(iq_ref, iw_ref, ik_ref, o_ref, key_ref, hi_ref, lo_ref, *, tq, tk, tko, nkb, topk, pos_bits):
    qi = pl.program_id(1)
    n_causal = ((qi + 1) * tq + tk - 1) // tk
    q_pos = qi * tq + lax.broadcasted_iota(jnp.int32, (1, tq), 1)
    sub = lax.broadcasted_iota(jnp.int32, (SUBLANES, tq), 0)
    iw = iw_ref[...]

    def score_body(kb, carry):
        k0 = pl.multiple_of(kb * tk, tk)
        for c in range(tk // IDX_KEYS):
            ikc = ik_ref[pl.ds(k0 + c * IDX_KEYS, IDX_KEYS), :]
            acc = jnp.zeros((IDX_KEYS, tq), F32)
            for h in range(IDX_HEADS):
                acc = acc + iw[h:h + 1, :] * jnp.maximum(_dot(ikc, iq_ref[h]), 0.0)
            bits = lax.bitcast_convert_type(acc, jnp.int32)
            key = bits ^ ((bits >> 31) & jnp.int32(0x7FFFFFFF))
            k_pos = k0 + c * IDX_KEYS + lax.broadcasted_iota(jnp.int32, (IDX_KEYS, 1), 0)
            key = jnp.where(k_pos <= q_pos, key, INT_MIN)
            cs = slice(c * IDX_KEYS, (c + 1) * IDX_KEYS)
            key_ref[kb, cs, :] = key
            hi_ref[kb, cs, :] = (key >> 16).astype(jnp.int16)
            lo_ref[kb, cs, :] = ((key & 0xFFFF) - HALF_RANGE).astype(jnp.int16)
        return carry

    lax.fori_loop(0, n_causal, score_body, 0)

    def count16(ref, pred):
        def body(kb, cs):
            cs = list(cs)
            for g in range(tk // PACKED_ROWS):
                hit = pred(ref[kb, g * PACKED_ROWS:(g + 1) * PACKED_ROWS, :])
                cs[g % len(cs)] = cs[g % len(cs)] + jnp.where(hit, jnp.int16(1), jnp.int16(0))
            return tuple(cs)
        zero = jnp.zeros((PACKED_ROWS, tq), jnp.int16)
        cs = lax.fori_loop(0, n_causal, body, (zero,) * IDX_ACCS)
        total = functools.reduce(jnp.add, cs).astype(jnp.int32)
        return jnp.broadcast_to(jnp.sum(total, axis=0, keepdims=True), (SUBLANES, tq))

    def packed_row(v):
        return jnp.broadcast_to(v[0:1, :], (PACKED_ROWS, tq)).astype(jnp.int16)

    def kth_largest16(ref, k):
        def step(it, thr):
            cand = thr + jnp.left_shift(jnp.int32(1), 15 - it)
            cand16 = packed_row(cand)
            return jnp.where(count16(ref, lambda v: v >= cand16) >= k, cand, thr)
        return lax.fori_loop(0, 16, step, jnp.full((SUBLANES, tq), -HALF_RANGE, jnp.int32))

    def count(pred):
        def body(kb, cs):
            cs = list(cs)
            for g in range(tk // SUBLANES):
                key = key_ref[kb, g * SUBLANES:(g + 1) * SUBLANES, :]
                hit = pred(key, kb * tk + g * SUBLANES + sub).astype(jnp.int32)
                cs[g % len(cs)] = cs[g % len(cs)] + hit
            return tuple(cs)
        zero = jnp.zeros((SUBLANES, tq), jnp.int32)
        cs = lax.fori_loop(0, n_causal, body, (zero,) * IDX_ACCS)
        total = functools.reduce(jnp.add, cs)
        return jnp.broadcast_to(jnp.sum(total, axis=0, keepdims=True), (SUBLANES, tq))

    thr_hi = kth_largest16(hi_ref, topk)
    thr_hi16 = packed_row(thr_hi)
    rank_lo = topk - count16(hi_ref, lambda v: v > thr_hi16)

    def mask_low(kb, carry):
        for g in range(tk // PACKED_ROWS):
            gs = slice(g * PACKED_ROWS, (g + 1) * PACKED_ROWS)
            lo_ref[kb, gs, :] = jnp.where(hi_ref[kb, gs, :] == thr_hi16, lo_ref[kb, gs, :], jnp.int16(-HALF_RANGE))
        return carry

    lax.fori_loop(0, n_causal, mask_low, 0)
    thr_lo = kth_largest16(lo_ref, rank_lo)
    thr = thr_hi * (2 * HALF_RANGE) + (thr_lo + HALF_RANGE)

    def tie_limit():
        keep = topk - count(lambda key, pos: key > thr)

        def step(it, below):
            cand = below + jnp.left_shift(jnp.int32(1), pos_bits - 1 - it)
            n_le = count(lambda key, pos: (key == thr) & (pos <= cand))
            return jnp.where(n_le < keep, cand, below)

        below = lax.fori_loop(0, pos_bits, step, jnp.full((SUBLANES, tq), -1, jnp.int32))
        return below + 1

    n_ge = count(lambda key, pos: key >= thr)
    has_excess_ties = jnp.max(n_ge.astype(F32)) > topk
    lim = lax.cond(has_excess_ties, tie_limit, lambda: jnp.full((SUBLANES, tq), 2 ** pos_bits, jnp.int32))
    thr_row, lim_row = thr[0:1, :], lim[0:1, :]

    def write_body(kb, carry):
        for j in range(tk // LANES):
            key = key_ref[kb, j * LANES:(j + 1) * LANES, :]
            pos = kb * tk + j * LANES + lax.broadcasted_iota(jnp.int32, (LANES, 1), 0)
            sel = ((key > thr_row) | ((key == thr_row) & (pos <= lim_row))) & (pos <= q_pos)
            bias = jnp.where(sel, 0.0, MASK_NEG)
            off = j * LANES
            o_ref[kb * (tk // tko) + off // tko, :, off % tko:off % tko + LANES] = bias.T.astype(o_ref.dtype)
        return carry

    lax.fori_loop(0, n_causal, write_body, 0)

    def fill_body(ob, carry):
        o_ref[ob] = jnp.full((tq, tko), MASK_NEG, o_ref.dtype)
        return carry

    lax.fori_loop(n_causal * (tk // tko), nkb * (tk // tko), fill_body, 0)


def _indexer(iq_t, iw_t, ik, batch, seq, tq, tk, tko):
    nkb = seq // tk
    nq = seq // tq
    topk = min(TOPK_MAX, seq // 4)
    assert tk % tko == 0 and tko % LANES == 0
    return pl.pallas_call(
        functools.partial(_indexer_kernel, tq=tq, tk=tk, tko=tko, nkb=nkb, topk=topk,
                          pos_bits=max(1, (seq - 1).bit_length())),
        grid=(batch, nq),
        in_specs=[pl.BlockSpec((IDX_HEADS, IDX_DIM, tq), lambda b, i: (0, 0, b * nq + i)),
                  pl.BlockSpec((IDX_HEADS, tq), lambda b, i: (0, b * nq + i)),
                  pl.BlockSpec((seq, IDX_DIM), lambda b, i: (b, 0))],
        out_specs=pl.BlockSpec((None, seq // tko, tq, tko), lambda b, i: (b, 0, i, 0)),
        out_shape=jax.ShapeDtypeStruct((batch, seq // tko, seq, tko), BF16),
        scratch_shapes=[pltpu.VMEM((nkb, tk, tq), jnp.int32),
                        pltpu.VMEM((nkb, tk, tq), jnp.int16),
                        pltpu.VMEM((nkb, tk, tq), jnp.int16)],
        compiler_params=_params(("parallel", "parallel"), 56),
        name="indexer",
    )(iq_t, iw_t, ik)


def _attn_kernel(q_ref, k_ref, v_ref, b_ref, o_ref, acc_ref, m_ref, l_ref, s_ref, p_ref, bias_ref, *, tq, tk):
    dh = C_HEAD_DIM
    group = C_HEADS // C_KV_HEADS
    qi, kb = pl.program_id(1), pl.program_id(2)
    last_kb = (qi * tq + tq - 1) // tk

    @pl.when(kb == 0)
    def _():
        m_ref[...] = jnp.full_like(m_ref, MASK_NEG)
        l_ref[...] = jnp.zeros_like(l_ref)
        acc_ref[...] = jnp.zeros_like(acc_ref)

    @pl.when(kb <= last_kb)
    def _():
        rows = group * tq
        bias_ref[...] = b_ref[...].astype(F32)

        def logits(g):
            qg = q_ref[g * group:(g + 1) * group].reshape(rows, dh)
            s_ref[g % 2] = _dot_nt(qg, k_ref[:, g * dh:(g + 1) * dh])

        logits(0)
        for g in range(C_KV_HEADS):
            if g + 1 < C_KV_HEADS:
                logits(g + 1)
            for c in range(rows // ATTN_ROWS):
                rs = slice(c * ATTN_ROWS, (c + 1) * ATTN_ROWS)
                brs = slice((c * ATTN_ROWS) % tq, (c * ATTN_ROWS) % tq + ATTN_ROWS)
                tiles = [s_ref[g % 2, rs, t * LANES:(t + 1) * LANES] + bias_ref[brs, t * LANES:(t + 1) * LANES]
                         for t in range(tk // LANES)]
                tile_max = functools.reduce(jnp.maximum, tiles)
                m_old = m_ref[g, rs, :]
                m_new = jnp.maximum(m_old, jnp.max(tile_max, axis=1, keepdims=True))
                alpha = jnp.exp2(m_old - m_new)
                ps = [jnp.exp2(t - m_new) for t in tiles]
                l_ref[g, rs, :] = alpha * l_ref[g, rs, :] + jnp.sum(functools.reduce(jnp.add, ps), axis=1,
                                                                  keepdims=True)
                m_ref[g, rs, :] = m_new
                acc_ref[g, rs, :] = alpha * acc_ref[g, rs, :]
                for t, p in enumerate(ps):
                    p_ref[g % 2, rs, t * LANES:(t + 1) * LANES] = p.astype(BF16)
            acc_ref[g] += _dot(p_ref[g % 2], v_ref[:, g * dh:(g + 1) * dh])

    @pl.when(kb == last_kb)
    def _():
        for g in range(C_KV_HEADS):
            out = acc_ref[g] / l_ref[g]
            for j in range(group):
                h = g * group + j
                o_ref[:, h * dh:(h + 1) * dh] = out[j * tq:(j + 1) * tq].astype(o_ref.dtype)


def _attention(q, k, v, mask, batch, seq, tq, tk):
    m = k.shape[0]
    nq, nkb = seq // tq, seq // tk
    qw, kw = C_HEADS * C_HEAD_DIM, C_KV_HEADS * C_HEAD_DIM
    rows = (C_HEADS // C_KV_HEADS) * tq
    clamp = lambda i, j: jnp.minimum(j, (i * tq + tq - 1) // tk)
    return pl.pallas_call(
        functools.partial(_attn_kernel, tq=tq, tk=tk),
        grid=(batch, nq, nkb),
        in_specs=[pl.BlockSpec((C_HEADS, tq, C_HEAD_DIM), lambda b, i, j: (0, b * nq + i, 0)),
                  pl.BlockSpec((tk, kw), lambda b, i, j: (b * nkb + clamp(i, j), 0)),
                  pl.BlockSpec((tk, kw), lambda b, i, j: (b * nkb + clamp(i, j), 0)),
                  pl.BlockSpec((None, None, tq, tk), lambda b, i, j: (b, clamp(i, j), i, 0))],
        out_specs=pl.BlockSpec((tq, qw), lambda b, i, j: (b * nq + i, 0)),
        out_shape=jax.ShapeDtypeStruct((m, qw), BF16),
        scratch_shapes=[pltpu.VMEM((C_KV_HEADS, rows, C_HEAD_DIM), F32),
                        pltpu.VMEM((C_KV_HEADS, rows, LANES), F32),
                        pltpu.VMEM((C_KV_HEADS, rows, LANES), F32),
                        pltpu.VMEM((2, rows, tk), F32),
                        pltpu.VMEM((2, rows, tk), BF16),
                        pltpu.VMEM((tq, tk), F32)],
        compiler_params=_params(("parallel", "parallel", "arbitrary"), 48),
        name="attention",
    )(q, k, v, mask)


def _odd_mixer(h, g, w_in, q_norm, k_norm, ln_g, ln_b, w_out, batch, seq):
    d = h.shape[1]
    n_main = (C_HEADS + 2 * C_KV_HEADS) * C_HEAD_DIM + IDX_HEADS * IDX_DIM
    n_misc = IDX_DIM + IDX_HEADS
    w_main = w_in[:, :n_main].astype(BF16)
    w_misc = jnp.zeros((d, LANES), F32).at[:, :n_misc].set(w_in[:, n_main:]).astype(BF16)
    main = _norm_matmul(h, g, w_main)
    misc = _norm_matmul(h, g, w_misc, tn=LANES)
    q, k, v, iq, ik, iw = _odd_prep(main, misc, q_norm, k_norm, ln_g, ln_b, seq)
    tq, tk_idx, tk_att = _tile(seq, 256), _tile(seq, 1024), _tile(seq, 512)
    mask = _indexer(iq, iw, ik, batch, seq, tq, tk_idx, tk_att)
    att = _attention(q, k, v, mask, batch, seq, tq, tk_att)
    return _matmul_res([att], [w_out.astype(BF16)], h)


def kernel(x, mix_norm, even_w_in, even_i_bias, even_f_bias, even_a_norm, even_b_norm, even_spatial,
           even_spatial_bias, even_w_out, odd_w_in, odd_q_norm, odd_k_norm, odd_idx_ln_g, odd_idx_ln_b,
           odd_w_out, ffn_norm, ffn_w_up, ffn_conv_w, ffn_conv_b, ffn_w_down):
    batch, seq, d = x.shape
    depth = mix_norm.shape[0]
    h = x.reshape(batch * seq, d)
    for layer in range(depth):
        j = layer // 2
        if layer % 2 == 0:
            h = _even_mixer(h, mix_norm[layer], even_w_in[j], even_i_bias[j], even_f_bias[j], even_a_norm[j],
                            even_b_norm[j], even_spatial[j], even_spatial_bias[j], even_w_out[j], batch, seq)
        else:
            h = _odd_mixer(h, mix_norm[layer], odd_w_in[j], odd_q_norm[j], odd_k_norm[j], odd_idx_ln_g[j],
                           odd_idx_ln_b[j], odd_w_out[j], batch, seq)
        h = _conv_ffn(h, ffn_norm[layer], ffn_w_up[layer], ffn_conv_w[layer], ffn_conv_b[layer],
                      ffn_w_down[layer], seq)
    return h.reshape(batch, seq, d)
```

```python
import functools

import jax
import jax.numpy as jnp
from jax import lax
from jax.experimental import pallas as pl
from jax.experimental.pallas import tpu as pltpu

A_HEADS = 4
A_HEAD_DIM = 256
B_GROUPS = 4
B_GROUP_DIM = 256
B_CHUNK = 128
C_HEADS = 16
C_KV_HEADS = 4
C_HEAD_DIM = 128
IDX_HEADS = 16
IDX_DIM = 64
IDX_ROPE_DIM = 32
TOPK_MAX = 256
ROPE_THETA = 10000.0
CONV_WIDTH = 3
NORM_EPS = 1e-6

A_WIDTH = A_HEADS * A_HEAD_DIM
B_WIDTH = B_GROUPS * B_GROUP_DIM

LANES = 128
HALO = 16
MLSTM_CHUNK = 256
ATTN_ROWS = 32
FFN_ROW_PARTS = 1
FFN_COL_PARTS = 2
SUBLANES = 8
PACKED_ROWS = 16
HALF_RANGE = 32768
IDX_KEYS = 256
IDX_ACCS = 4
MASK_NEG = -1e30
LOG2_E = 1.4426950408889634
INT_MIN = -(2 ** 31)

F32 = jnp.float32
BF16 = jnp.bfloat16


def _params(dims, vmem_mb):
    return pltpu.CompilerParams(dimension_semantics=dims, vmem_limit_bytes=vmem_mb * 1024 * 1024)


def _rms(x, g):
    return x * lax.rsqrt(jnp.mean(x * x, axis=-1, keepdims=True) + NORM_EPS) * g


def _log_sigmoid(x):
    return jnp.minimum(x, 0.0) - jnp.log1p(jnp.exp(-jnp.abs(x)))


def _dot(a, b):
    return jnp.dot(a, b, preferred_element_type=F32)


def _dot_nt(a, b):
    return lax.dot_general(a, b, (((1,), (1,)), ((), ())), preferred_element_type=F32)


def _dot_tn(a, b):
    return lax.dot_general(a, b, (((0,), (0,)), ((), ())), preferred_element_type=F32)


def _tile(n, pref):
    t = min(n, pref)
    assert n % t == 0, (n, pref)
    return t


def _norm_matmul_kernel(x_ref, g_ref, w_ref, o_ref, xn_ref):
    @pl.when(pl.program_id(1) == 0)
    def _():
        xn_ref[...] = _rms(x_ref[...], g_ref[...]).astype(BF16)

    o_ref[...] = _dot(xn_ref[...], w_ref[...]).astype(o_ref.dtype)


def _norm_matmul(x, g, w, out_dtype=F32, tm=1024, tn=1024):
    m, d = x.shape
    n = w.shape[1]
    tm, tn = _tile(m, tm), _tile(n, tn)
    return pl.pallas_call(
        _norm_matmul_kernel,
        grid=(m // tm, n // tn),
        in_specs=[pl.BlockSpec((tm, d), lambda i, j: (i, 0)),
                  pl.BlockSpec((1, d), lambda i, j: (0, 0)),
                  pl.BlockSpec((d, tn), lambda i, j: (0, j))],
        out_specs=pl.BlockSpec((tm, tn), lambda i, j: (i, j)),
        out_shape=jax.ShapeDtypeStruct((m, n), out_dtype),
        scratch_shapes=[pltpu.VMEM((tm, d), BF16)],
        compiler_params=_params(("parallel", "arbitrary"), 48),
        name="norm_matmul",
    )(x, g.reshape(1, d), w)


def _matmul_res_kernel(*refs, n_lhs):
    lhs, ws = refs[:n_lhs], refs[n_lhs:2 * n_lhs]
    res_ref, o_ref = refs[2 * n_lhs], refs[2 * n_lhs + 1]
    acc = res_ref[...]
    for a_ref, w_ref in zip(lhs, ws):
        acc = acc + _dot(a_ref[...], w_ref[...])
    o_ref[...] = acc


def _matmul_res(lhs_list, w_list, res, tm=1024, tn=1024):
    m, n = res.shape
    tm, tn = _tile(m, tm), _tile(n, tn)
    in_specs = ([pl.BlockSpec((tm, a.shape[1]), lambda i, j: (i, 0)) for a in lhs_list]
                + [pl.BlockSpec((w.shape[0], tn), lambda i, j: (0, j)) for w in w_list]
                + [pl.BlockSpec((tm, tn), lambda i, j: (i, j))])
    return pl.pallas_call(
        functools.partial(_matmul_res_kernel, n_lhs=len(lhs_list)),
        grid=(m // tm, n // tn),
        in_specs=in_specs,
        out_specs=pl.BlockSpec((tm, tn), lambda i, j: (i, j)),
        out_shape=jax.ShapeDtypeStruct((m, n), F32),
        compiler_params=_params(("parallel", "parallel"), 56),
        name="matmul_res",
    )(*lhs_list, *w_list, res)


def _ffn_up_kernel(x_ref, halo_ref, g_ref, wa_ref, wb_ref, cwa_ref, cwb_ref, o_ref, xn_ref, sa_ref, sb_ref,
                   *, tm, seq):
    i = pl.program_id(0)

    @pl.when(pl.program_id(1) == 0)
    def _():
        g = g_ref[...]
        starts_sequence = (i * tm) % seq == 0
        halo = jnp.where(starts_sequence, 0.0, _rms(halo_ref[...], g))
        xn_ref[0:HALO, :] = halo.astype(BF16)
        xn_ref[HALO:, :] = _rms(x_ref[...], g).astype(BF16)

    tn = o_ref.shape[1]
    rows, cols = tm // FFN_ROW_PARTS, tn // FFN_COL_PARTS

    def conv(s_ref, cw_ref, rp, cs):
        out = cw_ref[CONV_WIDTH:CONV_WIDTH + 1, cs]
        for tap in range(CONV_WIDTH):
            lo = HALO - (CONV_WIDTH - 1) + tap
            out = out + cw_ref[tap:tap + 1, cs] * s_ref[rp, lo:lo + rows, cs]
        return out

    for rp in range(FFN_ROW_PARTS):
        xn = xn_ref[rp * rows:(rp + 1) * rows + HALO, :]
        for cp in range(FFN_COL_PARTS):
            cs = slice(cp * cols, (cp + 1) * cols)
            sa_ref[rp, :, cs] = _dot(xn, wa_ref[:, cs])
            sb_ref[rp, :, cs] = _dot(xn, wb_ref[:, cs])
            a = conv(sa_ref, cwa_ref, rp, cs)
            b = conv(sb_ref, cwb_ref, rp, cs)
            o_ref[rp * rows:(rp + 1) * rows, cs] = (a * jax.nn.sigmoid(a) * b).astype(o_ref.dtype)


def _ffn_up(x, g, w_up, conv_w, conv_b, seq, tm=1024, tn=512):
    m, d = x.shape
    f = w_up.shape[1] // 2
    tm, tn = _tile(seq, tm), _tile(f, tn)
    nj = f // tn
    conv_wb = jnp.concatenate([conv_w, conv_b.reshape(1, 2 * f)], axis=0)
    return pl.pallas_call(
        functools.partial(_ffn_up_kernel, tm=tm, seq=seq),
        grid=(m // tm, nj),
        in_specs=[pl.BlockSpec((tm, d), lambda i, j: (i, 0)),
                  pl.BlockSpec((HALO, d), lambda i, j: (jnp.maximum(i * (tm // HALO) - 1, 0), 0)),
                  pl.BlockSpec((1, d), lambda i, j: (0, 0)),
                  pl.BlockSpec((d, tn), lambda i, j: (0, j)),
                  pl.BlockSpec((d, tn), lambda i, j: (0, j + nj)),
                  pl.BlockSpec((CONV_WIDTH + 1, tn), lambda i, j: (0, j)),
                  pl.BlockSpec((CONV_WIDTH + 1, tn), lambda i, j: (0, j + nj))],
        out_specs=pl.BlockSpec((tm, tn), lambda i, j: (i, j)),
        out_shape=jax.ShapeDtypeStruct((m, f), BF16),
        scratch_shapes=[pltpu.VMEM((tm + HALO, d), BF16),
                        pltpu.VMEM((FFN_ROW_PARTS, tm // FFN_ROW_PARTS + HALO, tn), F32),
                        pltpu.VMEM((FFN_ROW_PARTS, tm // FFN_ROW_PARTS + HALO, tn), F32)],
        compiler_params=_params(("parallel", "arbitrary"), 48),
        name="ffn_up",
    )(x, x, g.reshape(1, d), w_up, w_up, conv_wb, conv_wb)


def _conv_ffn(h, g, w_up, conv_w, conv_b, w_down, seq):
    gated = _ffn_up(h, g, w_up.astype(BF16), conv_w, conv_b, seq)
    return _matmul_res([gated], [w_down.astype(BF16)], h, tn=512)


def _mlstm_kernel(q_ref, k_ref, v_ref, o_ref, grow_ref, gcol_ref, bcol_ref, brow_ref, an_ref, out_ref,
                  c_ref, n_ref, m_ref, *, L):
    H, dh = A_HEADS, A_HEAD_DIM

    @pl.when(pl.program_id(1) == 0)
    def _():
        c_ref[...] = jnp.zeros_like(c_ref)
        n_ref[...] = jnp.zeros_like(n_ref)
        m_ref[...] = jnp.zeros_like(m_ref)

    row = lax.broadcasted_iota(jnp.int32, (L, L), 0)
    col = lax.broadcasted_iota(jnp.int32, (L, L), 1)
    lower = col <= row
    upper = row <= col
    grow = grow_ref[...] + bcol_ref[...]
    gcol = gcol_ref[...] + brow_ref[...]

    for h in range(H):
        hs = slice(h * dh, (h + 1) * dh)
        i_row, lf_row = grow[h:h + 1, :], _log_sigmoid(grow[H + h:H + h + 1, :])
        i_col, lf_col = gcol[:, h:h + 1], _log_sigmoid(gcol[:, H + h:H + h + 1])
        b_col = jnp.sum(jnp.where(lower, lf_row, 0.0), axis=1, keepdims=True)
        b_row = jnp.sum(jnp.where(upper, lf_col, 0.0), axis=0, keepdims=True)
        a_row = i_row - b_row
        a_col = i_col - b_col
        m_prev = m_ref[h][:, 0:1]
        m_col = jnp.maximum(m_prev, jnp.max(jnp.where(lower, a_row, -jnp.inf), axis=1, keepdims=True))
        e = jnp.where(lower, jnp.exp(a_row - m_col), 0.0)

        q = q_ref[:, hs].astype(F32) * (dh ** -0.5)
        kf = k_ref[:, hs].astype(F32)
        qb, kb, vb = q.astype(BF16), kf.astype(BF16), v_ref[:, hs].astype(BF16)
        s = _dot_nt(qb, kb) * e
        w_inter = jnp.exp(m_prev - m_col)
        num = _dot(s.astype(BF16), vb) + w_inter * _dot(qb, c_ref[h].astype(BF16))
        den = jnp.sum(s, axis=1, keepdims=True) + w_inter * jnp.sum(q * n_ref[h], axis=1, keepdims=True)
        hh = num / jnp.maximum(jnp.abs(den), jnp.exp(-(b_col + m_col)))

        m_last = m_col[L - 1:L, :]
        decay = jnp.exp(m_prev - m_last)
        wk = jnp.exp(a_col - m_last) * kf
        c_ref[h] = decay * c_ref[h] + _dot_tn(wk.astype(BF16), vb)
        n_ref[h] = decay * n_ref[h] + jnp.sum(wk, axis=0, keepdims=True)
        m_ref[h] = jnp.broadcast_to(b_col[L - 1:L, :] + m_last, (1, LANES))

        y = _rms(hh, an_ref[h:h + 1, :]) * jax.nn.sigmoid(o_ref[:, hs].astype(F32))
        out_ref[:, hs] = y.astype(out_ref.dtype)


def _mlstm(proj, gates, i_bias, f_bias, a_norm, batch, seq):
    m = proj.shape[0]
    L = _tile(seq, MLSTM_CHUNK)
    nc = seq // L
    H = A_HEADS
    nw = A_WIDTH // A_WIDTH
    del nw
    bias = jnp.concatenate([i_bias, f_bias]).astype(F32)
    bias_col = bias.reshape(2 * H, 1)
    bias_row = jnp.zeros((1, LANES), F32).at[0, :2 * H].set(bias)
    gates_row = gates[:, :2 * H].reshape(batch, seq, 2 * H).transpose(0, 2, 1)
    qkvo = lambda c: pl.BlockSpec((L, A_WIDTH), lambda b, t: (b * nc + t, c))
    return pl.pallas_call(
        functools.partial(_mlstm_kernel, L=L),
        grid=(batch, nc),
        in_specs=[qkvo(0), qkvo(1), qkvo(2), qkvo(3),
                  pl.BlockSpec((None, 2 * H, L), lambda b, t: (b, 0, t)),
                  pl.BlockSpec((L, LANES), lambda b, t: (b * nc + t, 0)),
                  pl.BlockSpec((2 * H, 1), lambda b, t: (0, 0)),
                  pl.BlockSpec((1, LANES), lambda b, t: (0, 0)),
                  pl.BlockSpec((H, A_HEAD_DIM), lambda b, t: (0, 0))],
        out_specs=pl.BlockSpec((L, A_WIDTH), lambda b, t: (b * nc + t, 0)),
        out_shape=jax.ShapeDtypeStruct((m, A_WIDTH), BF16),
        scratch_shapes=[pltpu.VMEM((H, A_HEAD_DIM, A_HEAD_DIM), F32),
                        pltpu.VMEM((H, 1, A_HEAD_DIM), F32),
                        pltpu.VMEM((H, 1, LANES), F32)],
        compiler_params=_params(("parallel", "arbitrary"), 48),
        name="mlstm",
    )(proj, proj, proj, proj, gates_row, gates, bias_col, bias_row, a_norm)


def _sgu_kernel(u_ref, z_ref, sp_ref, sb_ref, bn_ref, o_ref, *, chunks):
    T, gd = B_CHUNK, B_GROUP_DIM
    row = lax.broadcasted_iota(jnp.int32, (T, T), 0)
    col = lax.broadcasted_iota(jnp.int32, (T, T), 1)
    lower = col <= row
    for g in range(B_GROUPS):
        gs = slice(g * gd, (g + 1) * gd)
        w = jnp.where(lower, sp_ref[g], 0.0).astype(BF16)
        bias = sb_ref[:, g:g + 1]
        bn = bn_ref[g:g + 1, :]
        for c in range(chunks):
            rs = slice(c * T, (c + 1) * T)
            zn = _rms(jax.nn.gelu(z_ref[rs, gs].astype(F32)), bn)
            zmix = _dot(w, zn.astype(BF16)) + bias
            o_ref[rs, gs] = (jax.nn.gelu(u_ref[rs, gs].astype(F32)) * zmix).astype(o_ref.dtype)


def _sgu(proj, spatial, spatial_bias, b_norm, seq, u_block, z_block):
    m = proj.shape[0]
    tm = _tile(seq, 512)
    chunks = tm // B_CHUNK
    return pl.pallas_call(
        functools.partial(_sgu_kernel, chunks=chunks),
        grid=(m // tm,),
        in_specs=[pl.BlockSpec((tm, B_WIDTH), lambda i: (i, u_block)),
                  pl.BlockSpec((tm, B_WIDTH), lambda i: (i, z_block)),
                  pl.BlockSpec((B_GROUPS, B_CHUNK, B_CHUNK), lambda i: (0, 0, 0)),
                  pl.BlockSpec((B_CHUNK, B_GROUPS), lambda i: (0, 0)),
                  pl.BlockSpec((B_GROUPS, B_GROUP_DIM), lambda i: (0, 0))],
        out_specs=pl.BlockSpec((tm, B_WIDTH), lambda i: (i, 0)),
        out_shape=jax.ShapeDtypeStruct((m, B_WIDTH), BF16),
        compiler_params=_params(("parallel",), 48),
        name="sgu",
    )(proj, proj, spatial, spatial_bias.T, b_norm)


def _even_mixer(h, g, w_in, i_bias, f_bias, a_norm, b_norm, spatial, spatial_bias, w_out, batch, seq):
    d = h.shape[1]
    n_main = 4 * A_WIDTH
    w_main = jnp.concatenate([w_in[:, :n_main], w_in[:, n_main + 2 * A_HEADS:]], axis=1).astype(BF16)
    w_gate = jnp.zeros((d, LANES), F32).at[:, :2 * A_HEADS].set(w_in[:, n_main:n_main + 2 * A_HEADS]).astype(BF16)
    proj = _norm_matmul(h, g, w_main, out_dtype=BF16)
    gates = _norm_matmul(h, g, w_gate, tn=LANES)
    h_a = _mlstm(proj, gates, i_bias, f_bias, a_norm, batch, seq)
    blk = n_main // B_WIDTH
    h_b = _sgu(proj, spatial, spatial_bias, b_norm, seq, blk, blk + 1)
    w_out = w_out.astype(BF16)
    return _matmul_res([h_a, h_b], [w_out[:A_WIDTH], w_out[A_WIDTH:]], h)


def _rope_tables(seq):
    pos = jnp.arange(seq, dtype=jnp.int32).astype(F32)

    def angles(d):
        inv = jnp.power(jnp.float32(ROPE_THETA), -jnp.arange(0, d, 2, dtype=F32) / d)
        ang = pos[:, None] * inv[None, :]
        return jnp.cos(ang), jnp.sin(ang)

    cos, sin = angles(C_HEAD_DIM)
    cq = jnp.concatenate([cos, cos], axis=1)
    sq = jnp.concatenate([-sin, sin], axis=1)
    cos, sin = angles(IDX_ROPE_DIM)
    half = IDX_ROPE_DIM // 2
    rest = IDX_DIM - IDX_ROPE_DIM
    zeros_h, zeros_r = jnp.zeros((seq, half), F32), jnp.zeros((seq, rest), F32)
    ci = jnp.concatenate([cos, cos, jnp.ones((seq, rest), F32)], axis=1)
    s_from_left = jnp.concatenate([zeros_h, sin, zeros_r], axis=1)
    s_from_right = jnp.concatenate([-sin, zeros_h, zeros_r], axis=1)
    rep = LANES // IDX_DIM
    return cq, sq, jnp.tile(ci, (1, rep)), jnp.tile(s_from_left, (1, rep)), jnp.tile(s_from_right, (1, rep))


def _odd_prep_kernel(main_ref, misc_ref, cq_ref, sq_ref, ci_ref, sl_ref, sr_ref, qn_ref, kn_ref, lg_ref, lb_ref,
                     q_out, k_out, v_out, iq_out, ik_out, iw_out):
    dh = C_HEAD_DIM
    cq, sq = cq_ref[...], sq_ref[...]
    ci, sl, sr = ci_ref[...], sl_ref[...], sr_ref[...]
    half = IDX_ROPE_DIM // 2

    def rope_full(x):
        return x * cq + pltpu.roll(x, dh // 2, 1) * sq

    def rope_idx(x):
        return x * ci + pltpu.roll(x, half, 1) * sl + pltpu.roll(x, LANES - half, 1) * sr

    qn = qn_ref[...] * (dh ** -0.5 * LOG2_E)
    for h in range(C_HEADS):
        hs = slice(h * dh, (h + 1) * dh)
        q_out[h] = rope_full(_rms(main_ref[:, hs], qn)).astype(q_out.dtype)
    k0 = C_HEADS * dh
    for h in range(C_KV_HEADS):
        hs = slice(h * dh, (h + 1) * dh)
        src = slice(k0 + h * dh, k0 + (h + 1) * dh)
        k_out[:, hs] = rope_full(_rms(main_ref[:, src], kn_ref[...])).astype(k_out.dtype)
    v0 = k0 + C_KV_HEADS * dh
    v_out[...] = main_ref[:, v0:v0 + C_KV_HEADS * dh].astype(v_out.dtype)
    i0 = v0 + C_KV_HEADS * dh
    per_tile = LANES // IDX_DIM
    for t in range(IDX_HEADS // per_tile):
        y = rope_idx(main_ref[:, i0 + t * LANES:i0 + (t + 1) * LANES]).T
        for j in range(per_tile):
            iq_out[t * per_tile + j] = y[j * IDX_DIM:(j + 1) * IDX_DIM, :].astype(iq_out.dtype)

    misc = misc_ref[...]
    lane = lax.broadcasted_iota(jnp.int32, misc.shape, 1)
    is_key = lane < IDX_DIM
    mu = jnp.sum(jnp.where(is_key, misc, 0.0), axis=1, keepdims=True) / IDX_DIM
    cen = jnp.where(is_key, misc - mu, 0.0)
    var = jnp.sum(cen * cen, axis=1, keepdims=True) / IDX_DIM
    ikn = cen * lax.rsqrt(var + NORM_EPS) * lg_ref[...] + lb_ref[...]
    ik_out[...] = rope_idx(ikn)[:, :IDX_DIM].astype(ik_out.dtype)
    iw_out[...] = misc.T[IDX_DIM:IDX_DIM + IDX_HEADS, :] * (IDX_HEADS ** -0.5 * IDX_DIM ** -0.5)


def _odd_prep(main, misc, q_norm, k_norm, ln_g, ln_b, seq):
    m = main.shape[0]
    tm = _tile(seq, 256)
    ns = seq // tm
    tables = _rope_tables(seq)
    pad = lambda v: jnp.zeros((1, LANES), F32).at[0, :IDX_DIM].set(v)
    tab = pl.BlockSpec((tm, LANES), lambda i: (i % ns, 0))
    vec = pl.BlockSpec((1, LANES), lambda i: (0, 0))
    qw, kw = C_HEADS * C_HEAD_DIM, C_KV_HEADS * C_HEAD_DIM
    return pl.pallas_call(
        _odd_prep_kernel,
        grid=(m // tm,),
        in_specs=[pl.BlockSpec((tm, main.shape[1]), lambda i: (i, 0)),
                  pl.BlockSpec((tm, LANES), lambda i: (i, 0)),
                  tab, tab, tab, tab, tab, vec, vec, vec, vec],
        out_specs=[pl.BlockSpec((C_HEADS, tm, C_HEAD_DIM), lambda i: (0, i, 0)),
                   pl.BlockSpec((tm, kw), lambda i: (i, 0)),
                   pl.BlockSpec((tm, kw), lambda i: (i, 0)),
                   pl.BlockSpec((IDX_HEADS, IDX_DIM, tm), lambda i: (0, 0, i)),
                   pl.BlockSpec((tm, IDX_DIM), lambda i: (i, 0)),
                   pl.BlockSpec((IDX_HEADS, tm), lambda i: (0, i))],
        out_shape=[jax.ShapeDtypeStruct((C_HEADS, m, C_HEAD_DIM), BF16),
                   jax.ShapeDtypeStruct((m, kw), BF16),
                   jax.ShapeDtypeStruct((m, kw), BF16),
                   jax.ShapeDtypeStruct((IDX_HEADS, IDX_DIM, m), BF16),
                   jax.ShapeDtypeStruct((m, IDX_DIM), BF16),
                   jax.ShapeDtypeStruct((IDX_HEADS, m), F32)],
        compiler_params=_params(("parallel",), 48),
        name="odd_prep",
    )(main, misc, *tables, q_norm.reshape(1, LANES), k_norm.reshape(1, LANES), pad(ln_g), pad(ln_b))


def _indexer_kernel(iq_ref, iw_ref, ik_ref, o_ref, key_ref, hi_ref, lo_ref, *, tq, tk, tko, nkb, topk, pos_bits):
    qi = pl.program_id(1)
    n_causal = ((qi + 1) * tq + tk - 1) // tk
    q_pos = qi * tq + lax.broadcasted_iota(jnp.int32, (1, tq), 1)
    sub = lax.broadcasted_iota(jnp.int32, (SUBLANES, tq), 0)
    iw = iw_ref[...]

    def score_body(kb, carry):
        k0 = pl.multiple_of(kb * tk, tk)
        for c in range(tk // IDX_KEYS):
            ikc = ik_ref[pl.ds(k0 + c * IDX_KEYS, IDX_KEYS), :]
            acc = jnp.zeros((IDX_KEYS, tq), F32)
            for h in range(IDX_HEADS):
                acc = acc + iw[h:h + 1, :] * jnp.maximum(_dot(ikc, iq_ref[h]), 0.0)
            bits = lax.bitcast_convert_type(acc, jnp.int32)
            key = bits ^ ((bits >> 31) & jnp.int32(0x7FFFFFFF))
            k_pos = k0 + c * IDX_KEYS + lax.broadcasted_iota(jnp.int32, (IDX_KEYS, 1), 0)
            key = jnp.where(k_pos <= q_pos, key, INT_MIN)
            cs = slice(c * IDX_KEYS, (c + 1) * IDX_KEYS)
            key_ref[kb, cs, :] = key
            hi_ref[kb, cs, :] = (key >> 16).astype(jnp.int16)
            lo_ref[kb, cs, :] = ((key & 0xFFFF) - HALF_RANGE).astype(jnp.int16)
        return carry

    lax.fori_loop(0, n_causal, score_body, 0)

    def count16(ref, pred):
        def body(kb, cs):
            cs = list(cs)
            for g in range(tk // PACKED_ROWS):
                hit = pred(ref[kb, g * PACKED_ROWS:(g + 1) * PACKED_ROWS, :])
                cs[g % len(cs)] = cs[g % len(cs)] + jnp.where(hit, jnp.int16(1), jnp.int16(0))
            return tuple(cs)
        zero = jnp.zeros((PACKED_ROWS, tq), jnp.int16)
        cs = lax.fori_loop(0, n_causal, body, (zero,) * IDX_ACCS)
        total = functools.reduce(jnp.add, cs).astype(jnp.int32)
        return jnp.broadcast_to(jnp.sum(total, axis=0, keepdims=True), (SUBLANES, tq))

    def packed_row(v):
        return jnp.broadcast_to(v[0:1, :], (PACKED_ROWS, tq)).astype(jnp.int16)

    def kth_largest16(ref, k):
        def step(it, thr):
            cand = thr + jnp.left_shift(jnp.int32(1), 15 - it)
            cand16 = packed_row(cand)
            return jnp.where(count16(ref, lambda v: v >= cand16) >= k, cand, thr)
        return lax.fori_loop(0, 16, step, jnp.full((SUBLANES, tq), -HALF_RANGE, jnp.int32))

    def count(pred):
        def body(kb, cs):
            cs = list(cs)
            for g in range(tk // SUBLANES):
                key = key_ref[kb, g * SUBLANES:(g + 1) * SUBLANES, :]
                hit = pred(key, kb * tk + g * SUBLANES + sub).astype(jnp.int32)
                cs[g % len(cs)] = cs[g % len(cs)] + hit
            return tuple(cs)
        zero = jnp.zeros((SUBLANES, tq), jnp.int32)
        cs = lax.fori_loop(0, n_causal, body, (zero,) * IDX_ACCS)
        total = functools.reduce(jnp.add, cs)
        return jnp.broadcast_to(jnp.sum(total, axis=0, keepdims=True), (SUBLANES, tq))

    thr_hi = kth_largest16(hi_ref, topk)
    thr_hi16 = packed_row(thr_hi)
    rank_lo = topk - count16(hi_ref, lambda v: v > thr_hi16)

    def mask_low(kb, carry):
        for g in range(tk // PACKED_ROWS):
            gs = slice(g * PACKED_ROWS, (g + 1) * PACKED_ROWS)
            lo_ref[kb, gs, :] = jnp.where(hi_ref[kb, gs, :] == thr_hi16, lo_ref[kb, gs, :], jnp.int16(-HALF_RANGE))
        return carry

    lax.fori_loop(0, n_causal, mask_low, 0)
    thr_lo = kth_largest16(lo_ref, rank_lo)
    thr = thr_hi * (2 * HALF_RANGE) + (thr_lo + HALF_RANGE)

    def tie_limit():
        keep = topk - count(lambda key, pos: key > thr)

        def step(it, below):
            cand = below + jnp.left_shift(jnp.int32(1), pos_bits - 1 - it)
            n_le = count(lambda key, pos: (key == thr) & (pos <= cand))
            return jnp.where(n_le < keep, cand, below)

        below = lax.fori_loop(0, pos_bits, step, jnp.full((SUBLANES, tq), -1, jnp.int32))
        return below + 1

    n_ge = count(lambda key, pos: key >= thr)
    has_excess_ties = jnp.max(n_ge.astype(F32)) > topk
    lim = lax.cond(has_excess_ties, tie_limit, lambda: jnp.full((SUBLANES, tq), 2 ** pos_bits, jnp.int32))
    thr_row, lim_row = thr[0:1, :], lim[0:1, :]

    def write_body(kb, carry):
        for j in range(tk // LANES):
            key = key_ref[kb, j * LANES:(j + 1) * LANES, :]
            pos = kb * tk + j * LANES + lax.broadcasted_iota(jnp.int32, (LANES, 1), 0)
            sel = ((key > thr_row) | ((key == thr_row) & (pos <= lim_row))) & (pos <= q_pos)
            bias = jnp.where(sel, 0.0, MASK_NEG)
            off = j * LANES
            o_ref[kb * (tk // tko) + off // tko, :, off % tko:off % tko + LANES] = bias.T.astype(o_ref.dtype)
        return carry

    lax.fori_loop(0, n_causal, write_body, 0)

    def fill_body(ob, carry):
        o_ref[ob] = jnp.full((tq, tko), MASK_NEG, o_ref.dtype)
        return carry

    lax.fori_loop(n_causal * (tk // tko), nkb * (tk // tko), fill_body, 0)


def _indexer(iq_t, iw_t, ik, batch, seq, tq, tk, tko):
    nkb = seq // tk
    nq = seq // tq
    topk = min(TOPK_MAX, seq // 4)
    assert tk % tko == 0 and tko % LANES == 0
    return pl.pallas_call(
        functools.partial(_indexer_kernel, tq=tq, tk=tk, tko=tko, nkb=nkb, topk=topk,
                          pos_bits=max(1, (seq - 1).bit_length())),
        grid=(batch, nq),
        in_specs=[pl.BlockSpec((IDX_HEADS, IDX_DIM, tq), lambda b, i: (0, 0, b * nq + i)),
                  pl.BlockSpec((IDX_HEADS, tq), lambda b, i: (0, b * nq + i)),
                  pl.BlockSpec((seq, IDX_DIM), lambda b, i: (b, 0))],
        out_specs=pl.BlockSpec((None, seq // tko, tq, tko), lambda b, i: (b, 0, i, 0)),
        out_shape=jax.ShapeDtypeStruct((batch, seq // tko, seq, tko), BF16),
        scratch_shapes=[pltpu.VMEM((nkb, tk, tq), jnp.int32),
                        pltpu.VMEM((nkb, tk, tq), jnp.int16),
                        pltpu.VMEM((nkb, tk, tq), jnp.int16)],
        compiler_params=_params(("parallel", "parallel"), 56),
        name="indexer",
    )(iq_t, iw_t, ik)


def _attn_kernel(qi_ref, kb_ref, q_ref, k_ref, v_ref, b_ref, o_ref, acc_ref, m_ref, l_ref, s_ref, p_ref, bias_ref,
                 *, tq, tk):
    dh = C_HEAD_DIM
    group = C_HEADS // C_KV_HEADS
    pair = pl.program_id(1)
    qi, kb = qi_ref[pair], kb_ref[pair]
    last_kb = (qi * tq + tq - 1) // tk

    @pl.when(kb == 0)
    def _():
        m_ref[...] = jnp.full_like(m_ref, MASK_NEG)
        l_ref[...] = jnp.zeros_like(l_ref)
        acc_ref[...] = jnp.zeros_like(acc_ref)

    rows = group * tq
    bias_ref[...] = b_ref[...].astype(F32)

    def logits(g):
        qg = q_ref[g * group:(g + 1) * group].reshape(rows, dh)
        s_ref[g % 2] = _dot_nt(qg, k_ref[:, g * dh:(g + 1) * dh])

    logits(0)
    for g in range(C_KV_HEADS):
        if g + 1 < C_KV_HEADS:
            logits(g + 1)
        for c in range(rows // ATTN_ROWS):
            rs = slice(c * ATTN_ROWS, (c + 1) * ATTN_ROWS)
            brs = slice((c * ATTN_ROWS) % tq, (c * ATTN_ROWS) % tq + ATTN_ROWS)
            tiles = [s_ref[g % 2, rs, t * LANES:(t + 1) * LANES] + bias_ref[brs, t * LANES:(t + 1) * LANES]
                     for t in range(tk // LANES)]
            tile_max = functools.reduce(jnp.maximum, tiles)
            m_old = m_ref[g, rs, :]
            m_new = jnp.maximum(m_old, jnp.max(tile_max, axis=1, keepdims=True))
            alpha = jnp.exp2(m_old - m_new)
            ps = [jnp.exp2(t - m_new) for t in tiles]
            l_ref[g, rs, :] = alpha * l_ref[g, rs, :] + jnp.sum(functools.reduce(jnp.add, ps), axis=1,
                                                              keepdims=True)
            m_ref[g, rs, :] = m_new
            acc_ref[g, rs, :] = alpha * acc_ref[g, rs, :]
            for t, p in enumerate(ps):
                p_ref[g % 2, rs, t * LANES:(t + 1) * LANES] = p.astype(BF16)
        acc_ref[g] += _dot(p_ref[g % 2], v_ref[:, g * dh:(g + 1) * dh])

    @pl.when(kb == last_kb)
    def _():
        for g in range(C_KV_HEADS):
            out = acc_ref[g] / l_ref[g]
            for j in range(group):
                h = g * group + j
                o_ref[:, h * dh:(h + 1) * dh] = out[j * tq:(j + 1) * tq].astype(o_ref.dtype)


def _attention(q, k, v, mask, batch, seq, tq, tk):
    m = k.shape[0]
    nq, nkb = seq // tq, seq // tk
    qw, kw = C_HEADS * C_HEAD_DIM, C_KV_HEADS * C_HEAD_DIM
    rows = (C_HEADS // C_KV_HEADS) * tq
    pairs = [(i, j) for i in range(nq) for j in range((i * tq + tq - 1) // tk + 1)]
    qi_of = jnp.asarray([p[0] for p in pairs], jnp.int32)
    kb_of = jnp.asarray([p[1] for p in pairs], jnp.int32)
    grid_spec = pltpu.PrefetchScalarGridSpec(
        num_scalar_prefetch=2,
        grid=(batch, len(pairs)),
        in_specs=[pl.BlockSpec((C_HEADS, tq, C_HEAD_DIM), lambda b, p, qi, kb: (0, b * nq + qi[p], 0)),
                  pl.BlockSpec((tk, kw), lambda b, p, qi, kb: (b * nkb + kb[p], 0)),
                  pl.BlockSpec((tk, kw), lambda b, p, qi, kb: (b * nkb + kb[p], 0)),
                  pl.BlockSpec((None, None, tq, tk), lambda b, p, qi, kb: (b, kb[p], qi[p], 0))],
        out_specs=pl.BlockSpec((tq, qw), lambda b, p, qi, kb: (b * nq + qi[p], 0)),
        scratch_shapes=[pltpu.VMEM((C_KV_HEADS, rows, C_HEAD_DIM), F32),
                        pltpu.VMEM((C_KV_HEADS, rows, LANES), F32),
                        pltpu.VMEM((C_KV_HEADS, rows, LANES), F32),
                        pltpu.VMEM((2, rows, tk), F32),
                        pltpu.VMEM((2, rows, tk), BF16),
                        pltpu.VMEM((tq, tk), F32)])
    return pl.pallas_call(
        functools.partial(_attn_kernel, tq=tq, tk=tk),
        grid_spec=grid_spec,
        out_shape=jax.ShapeDtypeStruct((m, qw), BF16),
        compiler_params=_params(("parallel", "arbitrary"), 48),
        name="attention",
    )(qi_of, kb_of, q, k, v, mask)


def _odd_mixer(h, g, w_in, q_norm, k_norm, ln_g, ln_b, w_out, batch, seq):
    d = h.shape[1]
    n_main = (C_HEADS + 2 * C_KV_HEADS) * C_HEAD_DIM + IDX_HEADS * IDX_DIM
    n_misc = IDX_DIM + IDX_HEADS
    w_main = w_in[:, :n_main].astype(BF16)
    w_misc = jnp.zeros((d, LANES), F32).at[:, :n_misc].set(w_in[:, n_main:]).astype(BF16)
    main = _norm_matmul(h, g, w_main)
    misc = _norm_matmul(h, g, w_misc, tn=LANES)
    q, k, v, iq, ik, iw = _odd_prep(main, misc, q_norm, k_norm, ln_g, ln_b, seq)
    tq, tk_idx, tk_att = _tile(seq, 256), _tile(seq, 1024), _tile(seq, 512)
    mask = _indexer(iq, iw, ik, batch, seq, tq, tk_idx, tk_att)
    att = _attention(q, k, v, mask, batch, seq, tq, tk_att)
    return _matmul_res([att], [w_out.astype(BF16)], h)


def kernel(x, mix_norm, even_w_in, even_i_bias, even_f_bias, even_a_norm, even_b_norm, even_spatial,
           even_spatial_bias, even_w_out, odd_w_in, odd_q_norm, odd_k_norm, odd_idx_ln_g, odd_idx_ln_b,
           odd_w_out, ffn_norm, ffn_w_up, ffn_conv_w, ffn_conv_b, ffn_w_down):
    batch, seq, d = x.shape
    depth = mix_norm.shape[0]
    h = x.reshape(batch * seq, d)
    for layer in range(depth):
        j = layer // 2
        if layer % 2 == 0:
            h = _even_mixer(h, mix_norm[layer], even_w_in[j], even_i_bias[j], even_f_bias[j], even_a_norm[j],
                            even_b_norm[j], even_spatial[j], even_spatial_bias[j], even_w_out[j], batch, seq)
        else:
            h = _odd_mixer(h, mix_norm[layer], odd_w_in[j], odd_q_norm[j], odd_k_norm[j], odd_idx_ln_g[j],
                           odd_idx_ln_b[j], odd_w_out[j], batch, seq)
        h = _conv_ffn(h, ffn_norm[layer], ffn_w_up[layer], ffn_conv_w[layer], ffn_conv_b[layer],
                      ffn_w_down[layer], seq)
    return h.reshape(batch, seq, d)
```

```python
import functools

import jax
import jax.numpy as jnp
from jax import lax
from jax.experimental import pallas as pl
from jax.experimental.pallas import tpu as pltpu

A_HEADS = 4
A_HEAD_DIM = 256
B_GROUPS = 4
B_GROUP_DIM = 256
B_CHUNK = 128
C_HEADS = 16
C_KV_HEADS = 4
C_HEAD_DIM = 128
IDX_HEADS = 16
IDX_DIM = 64
IDX_ROPE_DIM = 32
TOPK_MAX = 256
ROPE_THETA = 10000.0
CONV_WIDTH = 3
NORM_EPS = 1e-6

A_WIDTH = A_HEADS * A_HEAD_DIM
B_WIDTH = B_GROUPS * B_GROUP_DIM

LANES = 128
HALO = 16
MLSTM_CHUNK = 256
ATTN_ROWS = 32
FFN_ROW_PARTS = 1
FFN_COL_PARTS = 2
SUBLANES = 8
PACKED_ROWS = 16
HALF_RANGE = 32768
IDX_KEYS = 256
IDX_ACCS = 4
MASK_NEG = -1e30
LOG2_E = 1.4426950408889634
INT_MIN = -(2 ** 31)

F32 = jnp.float32
BF16 = jnp.bfloat16


def _params(dims, vmem_mb):
    return pltpu.CompilerParams(dimension_semantics=dims, vmem_limit_bytes=vmem_mb * 1024 * 1024)


def _rms(x, g):
    return x * lax.rsqrt(jnp.mean(x * x, axis=-1, keepdims=True) + NORM_EPS) * g


def _log_sigmoid(x):
    return jnp.minimum(x, 0.0) - jnp.log1p(jnp.exp(-jnp.abs(x)))


def _dot(a, b):
    return jnp.dot(a, b, preferred_element_type=F32)


def _dot_nt(a, b):
    return lax.dot_general(a, b, (((1,), (1,)), ((), ())), preferred_element_type=F32)


def _dot_tn(a, b):
    return lax.dot_general(a, b, (((0,), (0,)), ((), ())), preferred_element_type=F32)


def _tile(n, pref):
    t = min(n, pref)
    assert n % t == 0, (n, pref)
    return t


def _norm_matmul_kernel(x_ref, g_ref, w_ref, o_ref, xn_ref):
    @pl.when(pl.program_id(1) == 0)
    def _():
        xn_ref[...] = _rms(x_ref[...], g_ref[...]).astype(BF16)

    o_ref[...] = _dot(xn_ref[...], w_ref[...]).astype(o_ref.dtype)


def _norm_matmul(x, g, w, out_dtype=F32, tm=1024, tn=1024):
    m, d = x.shape
    n = w.shape[1]
    tm, tn = _tile(m, tm), _tile(n, tn)
    return pl.pallas_call(
        _norm_matmul_kernel,
        grid=(m // tm, n // tn),
        in_specs=[pl.BlockSpec((tm, d), lambda i, j: (i, 0)),
                  pl.BlockSpec((1, d), lambda i, j: (0, 0)),
                  pl.BlockSpec((d, tn), lambda i, j: (0, j))],
        out_specs=pl.BlockSpec((tm, tn), lambda i, j: (i, j)),
        out_shape=jax.ShapeDtypeStruct((m, n), out_dtype),
        scratch_shapes=[pltpu.VMEM((tm, d), BF16)],
        compiler_params=_params(("parallel", "arbitrary"), 48),
        name="norm_matmul",
    )(x, g.reshape(1, d), w)


def _matmul_res_kernel(*refs, n_lhs):
    lhs, ws = refs[:n_lhs], refs[n_lhs:2 * n_lhs]
    res_ref, o_ref = refs[2 * n_lhs], refs[2 * n_lhs + 1]
    acc = res_ref[...]
    for a_ref, w_ref in zip(lhs, ws):
        acc = acc + _dot(a_ref[...], w_ref[...])
    o_ref[...] = acc


def _matmul_res(lhs_list, w_list, res, tm=1024, tn=1024):
    m, n = res.shape
    tm, tn = _tile(m, tm), _tile(n, tn)
    in_specs = ([pl.BlockSpec((tm, a.shape[1]), lambda i, j: (i, 0)) for a in lhs_list]
                + [pl.BlockSpec((w.shape[0], tn), lambda i, j: (0, j)) for w in w_list]
                + [pl.BlockSpec((tm, tn), lambda i, j: (i, j))])
    return pl.pallas_call(
        functools.partial(_matmul_res_kernel, n_lhs=len(lhs_list)),
        grid=(m // tm, n // tn),
        in_specs=in_specs,
        out_specs=pl.BlockSpec((tm, tn), lambda i, j: (i, j)),
        out_shape=jax.ShapeDtypeStruct((m, n), F32),
        compiler_params=_params(("parallel", "parallel"), 56),
        name="matmul_res",
    )(*lhs_list, *w_list, res)


def _ffn_up_kernel(x_ref, halo_ref, g_ref, wa_ref, wb_ref, cwa_ref, cwb_ref, o_ref, xn_ref, sa_ref, sb_ref,
                   *, tm, seq):
    i = pl.program_id(0)

    @pl.when(pl.program_id(1) == 0)
    def _():
        g = g_ref[...]
        starts_sequence = (i * tm) % seq == 0
        halo = jnp.where(starts_sequence, 0.0, _rms(halo_ref[...], g))
        xn_ref[0:HALO, :] = halo.astype(BF16)
        xn_ref[HALO:, :] = _rms(x_ref[...], g).astype(BF16)

    tn = o_ref.shape[1]
    rows, cols = tm // FFN_ROW_PARTS, tn // FFN_COL_PARTS

    def conv(s_ref, cw_ref, rp, cs):
        out = cw_ref[CONV_WIDTH:CONV_WIDTH + 1, cs]
        for tap in range(CONV_WIDTH):
            lo = HALO - (CONV_WIDTH - 1) + tap
            out = out + cw_ref[tap:tap + 1, cs] * s_ref[rp, lo:lo + rows, cs]
        return out

    for rp in range(FFN_ROW_PARTS):
        xn = xn_ref[rp * rows:(rp + 1) * rows + HALO, :]
        for cp in range(FFN_COL_PARTS):
            cs = slice(cp * cols, (cp + 1) * cols)
            sa_ref[rp, :, cs] = _dot(xn, wa_ref[:, cs])
            sb_ref[rp, :, cs] = _dot(xn, wb_ref[:, cs])
            a = conv(sa_ref, cwa_ref, rp, cs)
            b = conv(sb_ref, cwb_ref, rp, cs)
            half = 0.5 * a
            o_ref[rp * rows:(rp + 1) * rows, cs] = (half * (1.0 + jnp.tanh(half)) * b).astype(o_ref.dtype)


def _ffn_up(x, g, w_up, conv_w, conv_b, seq, tm=1024, tn=512):
    m, d = x.shape
    f = w_up.shape[1] // 2
    tm, tn = _tile(seq, tm), _tile(f, tn)
    nj = f // tn
    conv_wb = jnp.concatenate([conv_w, conv_b.reshape(1, 2 * f)], axis=0)
    return pl.pallas_call(
        functools.partial(_ffn_up_kernel, tm=tm, seq=seq),
        grid=(m // tm, nj),
        in_specs=[pl.BlockSpec((tm, d), lambda i, j: (i, 0)),
                  pl.BlockSpec((HALO, d), lambda i, j: (jnp.maximum(i * (tm // HALO) - 1, 0), 0)),
                  pl.BlockSpec((1, d), lambda i, j: (0, 0)),
                  pl.BlockSpec((d, tn), lambda i, j: (0, j)),
                  pl.BlockSpec((d, tn), lambda i, j: (0, j + nj)),
                  pl.BlockSpec((CONV_WIDTH + 1, tn), lambda i, j: (0, j)),
                  pl.BlockSpec((CONV_WIDTH + 1, tn), lambda i, j: (0, j + nj))],
        out_specs=pl.BlockSpec((tm, tn), lambda i, j: (i, j)),
        out_shape=jax.ShapeDtypeStruct((m, f), BF16),
        scratch_shapes=[pltpu.VMEM((tm + HALO, d), BF16),
                        pltpu.VMEM((FFN_ROW_PARTS, tm // FFN_ROW_PARTS + HALO, tn), F32),
                        pltpu.VMEM((FFN_ROW_PARTS, tm // FFN_ROW_PARTS + HALO, tn), F32)],
        compiler_params=_params(("parallel", "arbitrary"), 48),
        name="ffn_up",
    )(x, x, g.reshape(1, d), w_up, w_up, conv_wb, conv_wb)


def _conv_ffn(h, g, w_up, conv_w, conv_b, w_down, seq):
    gated = _ffn_up(h, g, w_up.astype(BF16), conv_w, conv_b, seq)
    return _matmul_res([gated], [w_down.astype(BF16)], h, tn=512)


def _mlstm_kernel(q_ref, k_ref, v_ref, o_ref, grow_ref, gcol_ref, bcol_ref, brow_ref, an_ref, out_ref,
                  c_ref, n_ref, m_ref, *, L):
    H, dh = A_HEADS, A_HEAD_DIM

    @pl.when(pl.program_id(1) == 0)
    def _():
        c_ref[...] = jnp.zeros_like(c_ref)
        n_ref[...] = jnp.zeros_like(n_ref)
        m_ref[...] = jnp.zeros_like(m_ref)

    row = lax.broadcasted_iota(jnp.int32, (L, L), 0)
    col = lax.broadcasted_iota(jnp.int32, (L, L), 1)
    lower = col <= row
    upper = row <= col
    grow = grow_ref[...] + bcol_ref[...]
    gcol = gcol_ref[...] + brow_ref[...]

    for h in range(H):
        hs = slice(h * dh, (h + 1) * dh)
        i_row, lf_row = grow[h:h + 1, :], _log_sigmoid(grow[H + h:H + h + 1, :])
        i_col, lf_col = gcol[:, h:h + 1], _log_sigmoid(gcol[:, H + h:H + h + 1])
        b_col = jnp.sum(jnp.where(lower, lf_row, 0.0), axis=1, keepdims=True)
        b_row = jnp.sum(jnp.where(upper, lf_col, 0.0), axis=0, keepdims=True)
        a_row = i_row - b_row
        a_col = i_col - b_col
        m_prev = m_ref[h][:, 0:1]
        m_col = jnp.maximum(m_prev, jnp.max(jnp.where(lower, a_row, -jnp.inf), axis=1, keepdims=True))
        e = jnp.where(lower, jnp.exp(a_row - m_col), 0.0)

        q = q_ref[:, hs].astype(F32) * (dh ** -0.5)
        kf = k_ref[:, hs].astype(F32)
        qb, kb, vb = q.astype(BF16), kf.astype(BF16), v_ref[:, hs].astype(BF16)
        s = _dot_nt(qb, kb) * e
        w_inter = jnp.exp(m_prev - m_col)
        num = _dot(s.astype(BF16), vb) + w_inter * _dot(qb, c_ref[h].astype(BF16))
        den = jnp.sum(s, axis=1, keepdims=True) + w_inter * jnp.sum(q * n_ref[h], axis=1, keepdims=True)
        hh = num / jnp.maximum(jnp.abs(den), jnp.exp(-(b_col + m_col)))

        m_last = m_col[L - 1:L, :]
        decay = jnp.exp(m_prev - m_last)
        wk = jnp.exp(a_col - m_last) * kf
        c_ref[h] = decay * c_ref[h] + _dot_tn(wk.astype(BF16), vb)
        n_ref[h] = decay * n_ref[h] + jnp.sum(wk, axis=0, keepdims=True)
        m_ref[h] = jnp.broadcast_to(b_col[L - 1:L, :] + m_last, (1, LANES))

        y = _rms(hh, an_ref[h:h + 1, :]) * jax.nn.sigmoid(o_ref[:, hs].astype(F32))
        out_ref[:, hs] = y.astype(out_ref.dtype)


def _mlstm(proj, gates, i_bias, f_bias, a_norm, batch, seq):
    m = proj.shape[0]
    L = _tile(seq, MLSTM_CHUNK)
    nc = seq // L
    H = A_HEADS
    nw = A_WIDTH // A_WIDTH
    del nw
    bias = jnp.concatenate([i_bias, f_bias]).astype(F32)
    bias_col = bias.reshape(2 * H, 1)
    bias_row = jnp.zeros((1, LANES), F32).at[0, :2 * H].set(bias)
    gates_row = gates[:, :2 * H].reshape(batch, seq, 2 * H).transpose(0, 2, 1)
    qkvo = lambda c: pl.BlockSpec((L, A_WIDTH), lambda b, t: (b * nc + t, c))
    return pl.pallas_call(
        functools.partial(_mlstm_kernel, L=L),
        grid=(batch, nc),
        in_specs=[qkvo(0), qkvo(1), qkvo(2), qkvo(3),
                  pl.BlockSpec((None, 2 * H, L), lambda b, t: (b, 0, t)),
                  pl.BlockSpec((L, LANES), lambda b, t: (b * nc + t, 0)),
                  pl.BlockSpec((2 * H, 1), lambda b, t: (0, 0)),
                  pl.BlockSpec((1, LANES), lambda b, t: (0, 0)),
                  pl.BlockSpec((H, A_HEAD_DIM), lambda b, t: (0, 0))],
        out_specs=pl.BlockSpec((L, A_WIDTH), lambda b, t: (b * nc + t, 0)),
        out_shape=jax.ShapeDtypeStruct((m, A_WIDTH), BF16),
        scratch_shapes=[pltpu.VMEM((H, A_HEAD_DIM, A_HEAD_DIM), F32),
                        pltpu.VMEM((H, 1, A_HEAD_DIM), F32),
                        pltpu.VMEM((H, 1, LANES), F32)],
        compiler_params=_params(("parallel", "arbitrary"), 48),
        name="mlstm",
    )(proj, proj, proj, proj, gates_row, gates, bias_col, bias_row, a_norm)


def _sgu_kernel(u_ref, z_ref, sp_ref, sb_ref, bn_ref, o_ref, *, chunks):
    T, gd = B_CHUNK, B_GROUP_DIM
    row = lax.broadcasted_iota(jnp.int32, (T, T), 0)
    col = lax.broadcasted_iota(jnp.int32, (T, T), 1)
    lower = col <= row
    for g in range(B_GROUPS):
        gs = slice(g * gd, (g + 1) * gd)
        w = jnp.where(lower, sp_ref[g], 0.0).astype(BF16)
        bias = sb_ref[:, g:g + 1]
        bn = bn_ref[g:g + 1, :]
        for c in range(chunks):
            rs = slice(c * T, (c + 1) * T)
            zn = _rms(jax.nn.gelu(z_ref[rs, gs].astype(F32)), bn)
            zmix = _dot(w, zn.astype(BF16)) + bias
            o_ref[rs, gs] = (jax.nn.gelu(u_ref[rs, gs].astype(F32)) * zmix).astype(o_ref.dtype)


def _sgu(proj, spatial, spatial_bias, b_norm, seq, u_block, z_block):
    m = proj.shape[0]
    tm = _tile(seq, 512)
    chunks = tm // B_CHUNK
    return pl.pallas_call(
        functools.partial(_sgu_kernel, chunks=chunks),
        grid=(m // tm,),
        in_specs=[pl.BlockSpec((tm, B_WIDTH), lambda i: (i, u_block)),
                  pl.BlockSpec((tm, B_WIDTH), lambda i: (i, z_block)),
                  pl.BlockSpec((B_GROUPS, B_CHUNK, B_CHUNK), lambda i: (0, 0, 0)),
                  pl.BlockSpec((B_CHUNK, B_GROUPS), lambda i: (0, 0)),
                  pl.BlockSpec((B_GROUPS, B_GROUP_DIM), lambda i: (0, 0))],
        out_specs=pl.BlockSpec((tm, B_WIDTH), lambda i: (i, 0)),
        out_shape=jax.ShapeDtypeStruct((m, B_WIDTH), BF16),
        compiler_params=_params(("parallel",), 48),
        name="sgu",
    )(proj, proj, spatial, spatial_bias.T, b_norm)


def _even_mixer(h, g, w_in, i_bias, f_bias, a_norm, b_norm, spatial, spatial_bias, w_out, batch, seq):
    d = h.shape[1]
    n_main = 4 * A_WIDTH
    w_main = jnp.concatenate([w_in[:, :n_main], w_in[:, n_main + 2 * A_HEADS:]], axis=1).astype(BF16)
    w_gate = jnp.zeros((d, LANES), F32).at[:, :2 * A_HEADS].set(w_in[:, n_main:n_main + 2 * A_HEADS]).astype(BF16)
    proj = _norm_matmul(h, g, w_main, out_dtype=BF16)
    gates = _norm_matmul(h, g, w_gate, tn=LANES)
    h_a = _mlstm(proj, gates, i_bias, f_bias, a_norm, batch, seq)
    blk = n_main // B_WIDTH
    h_b = _sgu(proj, spatial, spatial_bias, b_norm, seq, blk, blk + 1)
    w_out = w_out.astype(BF16)
    return _matmul_res([h_a, h_b], [w_out[:A_WIDTH], w_out[A_WIDTH:]], h)


def _rope_tables(seq):
    pos = jnp.arange(seq, dtype=jnp.int32).astype(F32)

    def angles(d):
        inv = jnp.power(jnp.float32(ROPE_THETA), -jnp.arange(0, d, 2, dtype=F32) / d)
        ang = pos[:, None] * inv[None, :]
        return jnp.cos(ang), jnp.sin(ang)

    cos, sin = angles(C_HEAD_DIM)
    cq = jnp.concatenate([cos, cos], axis=1)
    sq = jnp.concatenate([-sin, sin], axis=1)
    cos, sin = angles(IDX_ROPE_DIM)
    half = IDX_ROPE_DIM // 2
    rest = IDX_DIM - IDX_ROPE_DIM
    zeros_h, zeros_r = jnp.zeros((seq, half), F32), jnp.zeros((seq, rest), F32)
    ci = jnp.concatenate([cos, cos, jnp.ones((seq, rest), F32)], axis=1)
    s_from_left = jnp.concatenate([zeros_h, sin, zeros_r], axis=1)
    s_from_right = jnp.concatenate([-sin, zeros_h, zeros_r], axis=1)
    rep = LANES // IDX_DIM
    return cq, sq, jnp.tile(ci, (1, rep)), jnp.tile(s_from_left, (1, rep)), jnp.tile(s_from_right, (1, rep))


def _odd_prep_kernel(main_ref, misc_ref, cq_ref, sq_ref, ci_ref, sl_ref, sr_ref, qn_ref, kn_ref, lg_ref, lb_ref,
                     q_out, k_out, v_out, iq_out, ik_out, iw_out):
    dh = C_HEAD_DIM
    cq, sq = cq_ref[...], sq_ref[...]
    ci, sl, sr = ci_ref[...], sl_ref[...], sr_ref[...]
    half = IDX_ROPE_DIM // 2

    def rope_full(x):
        return x * cq + pltpu.roll(x, dh // 2, 1) * sq

    def rope_idx(x):
        return x * ci + pltpu.roll(x, half, 1) * sl + pltpu.roll(x, LANES - half, 1) * sr

    qn = qn_ref[...] * (dh ** -0.5 * LOG2_E)
    for h in range(C_HEADS):
        hs = slice(h * dh, (h + 1) * dh)
        q_out[h] = rope_full(_rms(main_ref[:, hs], qn)).astype(q_out.dtype)
    k0 = C_HEADS * dh
    for h in range(C_KV_HEADS):
        hs = slice(h * dh, (h + 1) * dh)
        src = slice(k0 + h * dh, k0 + (h + 1) * dh)
        k_out[:, hs] = rope_full(_rms(main_ref[:, src], kn_ref[...])).astype(k_out.dtype)
    v0 = k0 + C_KV_HEADS * dh
    v_out[...] = main_ref[:, v0:v0 + C_KV_HEADS * dh].astype(v_out.dtype)
    i0 = v0 + C_KV_HEADS * dh
    per_tile = LANES // IDX_DIM
    for t in range(IDX_HEADS // per_tile):
        y = rope_idx(main_ref[:, i0 + t * LANES:i0 + (t + 1) * LANES]).T
        for j in range(per_tile):
            iq_out[t * per_tile + j] = y[j * IDX_DIM:(j + 1) * IDX_DIM, :].astype(iq_out.dtype)

    misc = misc_ref[...]
    lane = lax.broadcasted_iota(jnp.int32, misc.shape, 1)
    is_key = lane < IDX_DIM
    mu = jnp.sum(jnp.where(is_key, misc, 0.0), axis=1, keepdims=True) / IDX_DIM
    cen = jnp.where(is_key, misc - mu, 0.0)
    var = jnp.sum(cen * cen, axis=1, keepdims=True) / IDX_DIM
    ikn = cen * lax.rsqrt(var + NORM_EPS) * lg_ref[...] + lb_ref[...]
    ik_out[...] = rope_idx(ikn)[:, :IDX_DIM].astype(ik_out.dtype)
    iw_out[...] = misc.T[IDX_DIM:IDX_DIM + IDX_HEADS, :] * (IDX_HEADS ** -0.5 * IDX_DIM ** -0.5)


def _odd_prep(main, misc, q_norm, k_norm, ln_g, ln_b, seq):
    m = main.shape[0]
    tm = _tile(seq, 256)
    ns = seq // tm
    tables = _rope_tables(seq)
    pad = lambda v: jnp.zeros((1, LANES), F32).at[0, :IDX_DIM].set(v)
    tab = pl.BlockSpec((tm, LANES), lambda i: (i % ns, 0))
    vec = pl.BlockSpec((1, LANES), lambda i: (0, 0))
    qw, kw = C_HEADS * C_HEAD_DIM, C_KV_HEADS * C_HEAD_DIM
    return pl.pallas_call(
        _odd_prep_kernel,
        grid=(m // tm,),
        in_specs=[pl.BlockSpec((tm, main.shape[1]), lambda i: (i, 0)),
                  pl.BlockSpec((tm, LANES), lambda i: (i, 0)),
                  tab, tab, tab, tab, tab, vec, vec, vec, vec],
        out_specs=[pl.BlockSpec((C_HEADS, tm, C_HEAD_DIM), lambda i: (0, i, 0)),
                   pl.BlockSpec((tm, kw), lambda i: (i, 0)),
                   pl.BlockSpec((tm, kw), lambda i: (i, 0)),
                   pl.BlockSpec((IDX_HEADS, IDX_DIM, tm), lambda i: (0, 0, i)),
                   pl.BlockSpec((tm, IDX_DIM), lambda i: (i, 0)),
                   pl.BlockSpec((IDX_HEADS, tm), lambda i: (0, i))],
        out_shape=[jax.ShapeDtypeStruct((C_HEADS, m, C_HEAD_DIM), BF16),
                   jax.ShapeDtypeStruct((m, kw), BF16),
                   jax.ShapeDtypeStruct((m, kw), BF16),
                   jax.ShapeDtypeStruct((IDX_HEADS, IDX_DIM, m), BF16),
                   jax.ShapeDtypeStruct((m, IDX_DIM), BF16),
                   jax.ShapeDtypeStruct((IDX_HEADS, m), F32)],
        compiler_params=_params(("parallel",), 48),
        name="odd_prep",
    )(main, misc, *tables, q_norm.reshape(1, LANES), k_norm.reshape(1, LANES), pad(ln_g), pad(ln_b))


def _indexer_kernel(iq_ref, iw_ref, ik_ref, o_ref, key_ref, hi_ref, lo_ref, *, tq, tk, tko, nkb, topk, pos_bits):
    qi = pl.program_id(1)
    n_causal = ((qi + 1) * tq + tk - 1) // tk
    q_pos = qi * tq + lax.broadcasted_iota(jnp.int32, (1, tq), 1)
    sub = lax.broadcasted_iota(jnp.int32, (SUBLANES, tq), 0)
    iw = iw_ref[...]

    def score_body(kb, carry):
        k0 = pl.multiple_of(kb * tk, tk)
        for c in range(tk // IDX_KEYS):
            ikc = ik_ref[pl.ds(k0 + c * IDX_KEYS, IDX_KEYS), :]
            acc = jnp.zeros((IDX_KEYS, tq), F32)
            for h in range(IDX_HEADS):
                acc = acc + iw[h:h + 1, :] * jnp.maximum(_dot(ikc, iq_ref[h]), 0.0)
            bits = lax.bitcast_convert_type(acc, jnp.int32)
            key = bits ^ ((bits >> 31) & jnp.int32(0x7FFFFFFF))
            k_pos = k0 + c * IDX_KEYS + lax.broadcasted_iota(jnp.int32, (IDX_KEYS, 1), 0)
            key = jnp.where(k_pos <= q_pos, key, INT_MIN)
            cs = slice(c * IDX_KEYS, (c + 1) * IDX_KEYS)
            key_ref[kb, cs, :] = key
            hi_ref[kb, cs, :] = (key >> 16).astype(jnp.int16)
            lo_ref[kb, cs, :] = ((key & 0xFFFF) - HALF_RANGE).astype(jnp.int16)
        return carry

    lax.fori_loop(0, n_causal, score_body, 0)

    def count16(ref, pred):
        def body(kb, cs):
            cs = list(cs)
            for g in range(tk // PACKED_ROWS):
                hit = pred(ref[kb, g * PACKED_ROWS:(g + 1) * PACKED_ROWS, :])
                cs[g % len(cs)] = cs[g % len(cs)] + jnp.where(hit, jnp.int16(1), jnp.int16(0))
            return tuple(cs)
        zero = jnp.zeros((PACKED_ROWS, tq), jnp.int16)
        cs = lax.fori_loop(0, n_causal, body, (zero,) * IDX_ACCS)
        total = functools.reduce(jnp.add, cs).astype(jnp.int32)
        return jnp.broadcast_to(jnp.sum(total, axis=0, keepdims=True), (SUBLANES, tq))

    def packed_row(v):
        return jnp.broadcast_to(v[0:1, :], (PACKED_ROWS, tq)).astype(jnp.int16)

    def kth_largest16(ref, k):
        def step(it, thr):
            cand = thr + jnp.left_shift(jnp.int32(1), 15 - it)
            cand16 = packed_row(cand)
            return jnp.where(count16(ref, lambda v: v >= cand16) >= k, cand, thr)
        return lax.fori_loop(0, 16, step, jnp.full((SUBLANES, tq), -HALF_RANGE, jnp.int32))

    def count(pred):
        def body(kb, cs):
            cs = list(cs)
            for g in range(tk // SUBLANES):
                key = key_ref[kb, g * SUBLANES:(g + 1) * SUBLANES, :]
                hit = pred(key, kb * tk + g * SUBLANES + sub).astype(jnp.int32)
                cs[g % len(cs)] = cs[g % len(cs)] + hit
            return tuple(cs)
        zero = jnp.zeros((SUBLANES, tq), jnp.int32)
        cs = lax.fori_loop(0, n_causal, body, (zero,) * IDX_ACCS)
        total = functools.reduce(jnp.add, cs)
        return jnp.broadcast_to(jnp.sum(total, axis=0, keepdims=True), (SUBLANES, tq))

    thr_hi = kth_largest16(hi_ref, topk)
    thr_hi16 = packed_row(thr_hi)
    rank_lo = topk - count16(hi_ref, lambda v: v > thr_hi16)

    def mask_low(kb, carry):
        for g in range(tk // PACKED_ROWS):
            gs = slice(g * PACKED_ROWS, (g + 1) * PACKED_ROWS)
            lo_ref[kb, gs, :] = jnp.where(hi_ref[kb, gs, :] == thr_hi16, lo_ref[kb, gs, :], jnp.int16(-HALF_RANGE))
        return carry

    lax.fori_loop(0, n_causal, mask_low, 0)
    thr_lo = kth_largest16(lo_ref, rank_lo)
    thr = thr_hi * (2 * HALF_RANGE) + (thr_lo + HALF_RANGE)

    def tie_limit():
        keep = topk - count(lambda key, pos: key > thr)

        def step(it, below):
            cand = below + jnp.left_shift(jnp.int32(1), pos_bits - 1 - it)
            n_le = count(lambda key, pos: (key == thr) & (pos <= cand))
            return jnp.where(n_le < keep, cand, below)

        below = lax.fori_loop(0, pos_bits, step, jnp.full((SUBLANES, tq), -1, jnp.int32))
        return below + 1

    n_ge = count(lambda key, pos: key >= thr)
    has_excess_ties = jnp.max(n_ge.astype(F32)) > topk
    lim = lax.cond(has_excess_ties, tie_limit, lambda: jnp.full((SUBLANES, tq), 2 ** pos_bits, jnp.int32))
    thr_row, lim_row = thr[0:1, :], lim[0:1, :]

    def write_body(kb, carry):
        for j in range(tk // LANES):
            key = key_ref[kb, j * LANES:(j + 1) * LANES, :]
            pos = kb * tk + j * LANES + lax.broadcasted_iota(jnp.int32, (LANES, 1), 0)
            sel = ((key > thr_row) | ((key == thr_row) & (pos <= lim_row))) & (pos <= q_pos)
            bias = jnp.where(sel, 0.0, MASK_NEG)
            off = j * LANES
            o_ref[kb * (tk // tko) + off // tko, :, off % tko:off % tko + LANES] = bias.T.astype(o_ref.dtype)
        return carry

    lax.fori_loop(0, n_causal, write_body, 0)

    def fill_body(ob, carry):
        o_ref[ob] = jnp.full((tq, tko), MASK_NEG, o_ref.dtype)
        return carry

    lax.fori_loop(n_causal * (tk // tko), nkb * (tk // tko), fill_body, 0)


def _indexer(iq_t, iw_t, ik, batch, seq, tq, tk, tko):
    nkb = seq // tk
    nq = seq // tq
    topk = min(TOPK_MAX, seq // 4)
    assert tk % tko == 0 and tko % LANES == 0
    return pl.pallas_call(
        functools.partial(_indexer_kernel, tq=tq, tk=tk, tko=tko, nkb=nkb, topk=topk,
                          pos_bits=max(1, (seq - 1).bit_length())),
        grid=(batch, nq),
        in_specs=[pl.BlockSpec((IDX_HEADS, IDX_DIM, tq), lambda b, i: (0, 0, b * nq + i)),
                  pl.BlockSpec((IDX_HEADS, tq), lambda b, i: (0, b * nq + i)),
                  pl.BlockSpec((seq, IDX_DIM), lambda b, i: (b, 0))],
        out_specs=pl.BlockSpec((None, seq // tko, tq, tko), lambda b, i: (b, 0, i, 0)),
        out_shape=jax.ShapeDtypeStruct((batch, seq // tko, seq, tko), BF16),
        scratch_shapes=[pltpu.VMEM((nkb, tk, tq), jnp.int32),
                        pltpu.VMEM((nkb, tk, tq), jnp.int16),
                        pltpu.VMEM((nkb, tk, tq), jnp.int16)],
        compiler_params=_params(("parallel", "parallel"), 56),
        name="indexer",
    )(iq_t, iw_t, ik)


def _attn_kernel(qi_ref, kb_ref, q_ref, k_ref, v_ref, b_ref, o_ref, acc_ref, m_ref, l_ref, s_ref, p_ref, bias_ref,
                 *, tq, tk):
    dh = C_HEAD_DIM
    group = C_HEADS // C_KV_HEADS
    pair = pl.program_id(1)
    qi, kb = qi_ref[pair], kb_ref[pair]
    last_kb = (qi * tq + tq - 1) // tk

    @pl.when(kb == 0)
    def _():
        m_ref[...] = jnp.full_like(m_ref, MASK_NEG)
        l_ref[...] = jnp.zeros_like(l_ref)
        acc_ref[...] = jnp.zeros_like(acc_ref)

    rows = group * tq
    bias_ref[...] = b_ref[...].astype(F32)

    def logits(g):
        qg = q_ref[g * group:(g + 1) * group].reshape(rows, dh)
        s_ref[g % 2] = _dot_nt(qg, k_ref[:, g * dh:(g + 1) * dh])

    logits(0)
    for g in range(C_KV_HEADS):
        if g + 1 < C_KV_HEADS:
            logits(g + 1)
        for c in range(rows // ATTN_ROWS):
            rs = slice(c * ATTN_ROWS, (c + 1) * ATTN_ROWS)
            brs = slice((c * ATTN_ROWS) % tq, (c * ATTN_ROWS) % tq + ATTN_ROWS)
            tiles = [s_ref[g % 2, rs, t * LANES:(t + 1) * LANES] + bias_ref[brs, t * LANES:(t + 1) * LANES]
                     for t in range(tk // LANES)]
            tile_max = functools.reduce(jnp.maximum, tiles)
            m_old = m_ref[g, rs, :]
            m_new = jnp.maximum(m_old, jnp.max(tile_max, axis=1, keepdims=True))
            alpha = jnp.exp2(m_old - m_new)
            ps = [jnp.exp2(t - m_new) for t in tiles]
            l_ref[g, rs, :] = alpha * l_ref[g, rs, :] + jnp.sum(functools.reduce(jnp.add, ps), axis=1,
                                                              keepdims=True)
            m_ref[g, rs, :] = m_new
            acc_ref[g, rs, :] = alpha * acc_ref[g, rs, :]
            for t, p in enumerate(ps):
                p_ref[g % 2, rs, t * LANES:(t + 1) * LANES] = p.astype(BF16)
        acc_ref[g] += _dot(p_ref[g % 2], v_ref[:, g * dh:(g + 1) * dh])

    @pl.when(kb == last_kb)
    def _():
        for g in range(C_KV_HEADS):
            out = acc_ref[g] / l_ref[g]
            for j in range(group):
                h = g * group + j
                o_ref[:, h * dh:(h + 1) * dh] = out[j * tq:(j + 1) * tq].astype(o_ref.dtype)


def _attention(q, k, v, mask, batch, seq, tq, tk):
    m = k.shape[0]
    nq, nkb = seq // tq, seq // tk
    qw, kw = C_HEADS * C_HEAD_DIM, C_KV_HEADS * C_HEAD_DIM
    rows = (C_HEADS // C_KV_HEADS) * tq
    pairs = [(i, j) for i in range(nq) for j in range((i * tq + tq - 1) // tk + 1)]
    qi_of = jnp.asarray([p[0] for p in pairs], jnp.int32)
    kb_of = jnp.asarray([p[1] for p in pairs], jnp.int32)
    grid_spec = pltpu.PrefetchScalarGridSpec(
        num_scalar_prefetch=2,
        grid=(batch, len(pairs)),
        in_specs=[pl.BlockSpec((C_HEADS, tq, C_HEAD_DIM), lambda b, p, qi, kb: (0, b * nq + qi[p], 0)),
                  pl.BlockSpec((tk, kw), lambda b, p, qi, kb: (b * nkb + kb[p], 0)),
                  pl.BlockSpec((tk, kw), lambda b, p, qi, kb: (b * nkb + kb[p], 0)),
                  pl.BlockSpec((None, None, tq, tk), lambda b, p, qi, kb: (b, kb[p], qi[p], 0))],
        out_specs=pl.BlockSpec((tq, qw), lambda b, p, qi, kb: (b * nq + qi[p], 0)),
        scratch_shapes=[pltpu.VMEM((C_KV_HEADS, rows, C_HEAD_DIM), F32),
                        pltpu.VMEM((C_KV_HEADS, rows, LANES), F32),
                        pltpu.VMEM((C_KV_HEADS, rows, LANES), F32),
                        pltpu.VMEM((2, rows, tk), F32),
                        pltpu.VMEM((2, rows, tk), BF16),
                        pltpu.VMEM((tq, tk), F32)])
    return pl.pallas_call(
        functools.partial(_attn_kernel, tq=tq, tk=tk),
        grid_spec=grid_spec,
        out_shape=jax.ShapeDtypeStruct((m, qw), BF16),
        compiler_params=_params(("parallel", "arbitrary"), 48),
        name="attention",
    )(qi_of, kb_of, q, k, v, mask)


def _odd_mixer(h, g, w_in, q_norm, k_norm, ln_g, ln_b, w_out, batch, seq):
    d = h.shape[1]
    n_main = (C_HEADS + 2 * C_KV_HEADS) * C_HEAD_DIM + IDX_HEADS * IDX_DIM
    n_misc = IDX_DIM + IDX_HEADS
    w_main = w_in[:, :n_main].astype(BF16)
    w_misc = jnp.zeros((d, LANES), F32).at[:, :n_misc].set(w_in[:, n_main:]).astype(BF16)
    main = _norm_matmul(h, g, w_main)
    misc = _norm_matmul(h, g, w_misc, tn=LANES)
    q, k, v, iq, ik, iw = _odd_prep(main, misc, q_norm, k_norm, ln_g, ln_b, seq)
    tq, tk_idx, tk_att = _tile(seq, 256), _tile(seq, 512), _tile(seq, 512)
    mask = _indexer(iq, iw, ik, batch, seq, tq, tk_idx, tk_att)
    att = _attention(q, k, v, mask, batch, seq, tq, tk_att)
    return _matmul_res([att], [w_out.astype(BF16)], h)


def kernel(x, mix_norm, even_w_in, even_i_bias, even_f_bias, even_a_norm, even_b_norm, even_spatial,
           even_spatial_bias, even_w_out, odd_w_in, odd_q_norm, odd_k_norm, odd_idx_ln_g, odd_idx_ln_b,
           odd_w_out, ffn_norm, ffn_w_up, ffn_conv_w, ffn_conv_b, ffn_w_down):
    batch, seq, d = x.shape
    depth = mix_norm.shape[0]
    h = x.reshape(batch * seq, d)
    for layer in range(depth):
        j = layer // 2
        if layer % 2 == 0:
            h = _even_mixer(h, mix_norm[layer], even_w_in[j], even_i_bias[j], even_f_bias[j], even_a_norm[j],
                            even_b_norm[j], even_spatial[j], even_spatial_bias[j], even_w_out[j], batch, seq)
        else:
            h = _odd_mixer(h, mix_norm[layer], odd_w_in[j], odd_q_norm[j], odd_k_norm[j], odd_idx_ln_g[j],
                           odd_idx_ln_b[j], odd_w_out[j], batch, seq)
        h = _conv_ffn(h, ffn_norm[layer], ffn_w_up[layer], ffn_conv_w[layer], ffn_conv_b[layer],
                      ffn_w_down[layer], seq)
    return h.reshape(batch, seq, d)
```

```python
import functools

import jax
import jax.numpy as jnp
from jax import lax
from jax.experimental import pallas as pl
from jax.experimental.pallas import tpu as pltpu

A_HEADS = 4
A_HEAD_DIM = 256
B_GROUPS = 4
B_GROUP_DIM = 256
B_CHUNK = 128
C_HEADS = 16
C_KV_HEADS = 4
C_HEAD_DIM = 128
IDX_HEADS = 16
IDX_DIM = 64
IDX_ROPE_DIM = 32
TOPK_MAX = 256
ROPE_THETA = 10000.0
CONV_WIDTH = 3
NORM_EPS = 1e-6

A_WIDTH = A_HEADS * A_HEAD_DIM
B_WIDTH = B_GROUPS * B_GROUP_DIM

LANES = 128
HALO = 16
MLSTM_CHUNK = 256
ATTN_ROWS = 32
FFN_ROW_PARTS = 2
FFN_COL_PARTS = 2
SUBLANES = 8
PACKED_ROWS = 16
HALF_RANGE = 32768
IDX_KEYS = 256
IDX_ACCS = 4
MASK_NEG = -1e30
LOG2_E = 1.4426950408889634
INT_MIN = -(2 ** 31)

F32 = jnp.float32
BF16 = jnp.bfloat16


def _params(dims, vmem_mb):
    return pltpu.CompilerParams(dimension_semantics=dims, vmem_limit_bytes=vmem_mb * 1024 * 1024)


def _rms(x, g):
    return x * lax.rsqrt(jnp.mean(x * x, axis=-1, keepdims=True) + NORM_EPS) * g


def _log_sigmoid(x):
    return jnp.minimum(x, 0.0) - jnp.log1p(jnp.exp(-jnp.abs(x)))


def _dot(a, b):
    return jnp.dot(a, b, preferred_element_type=F32)


def _dot_nt(a, b):
    return lax.dot_general(a, b, (((1,), (1,)), ((), ())), preferred_element_type=F32)


def _dot_tn(a, b):
    return lax.dot_general(a, b, (((0,), (0,)), ((), ())), preferred_element_type=F32)


def _tile(n, pref):
    t = min(n, pref)
    assert n % t == 0, (n, pref)
    return t


def _norm_matmul_kernel(x_ref, g_ref, w_ref, o_ref, xn_ref):
    @pl.when(pl.program_id(1) == 0)
    def _():
        xn_ref[...] = _rms(x_ref[...], g_ref[...]).astype(BF16)

    o_ref[...] = _dot(xn_ref[...], w_ref[...]).astype(o_ref.dtype)


def _norm_matmul(x, g, w, out_dtype=F32, tm=1024, tn=1024):
    m, d = x.shape
    n = w.shape[1]
    tm, tn = _tile(m, tm), _tile(n, tn)
    return pl.pallas_call(
        _norm_matmul_kernel,
        grid=(m // tm, n // tn),
        in_specs=[pl.BlockSpec((tm, d), lambda i, j: (i, 0)),
                  pl.BlockSpec((1, d), lambda i, j: (0, 0)),
                  pl.BlockSpec((d, tn), lambda i, j: (0, j))],
        out_specs=pl.BlockSpec((tm, tn), lambda i, j: (i, j)),
        out_shape=jax.ShapeDtypeStruct((m, n), out_dtype),
        scratch_shapes=[pltpu.VMEM((tm, d), BF16)],
        compiler_params=_params(("parallel", "arbitrary"), 48),
        name="norm_matmul",
    )(x, g.reshape(1, d), w)


def _matmul_res_kernel(*refs, n_lhs):
    lhs, ws = refs[:n_lhs], refs[n_lhs:2 * n_lhs]
    res_ref, o_ref = refs[2 * n_lhs], refs[2 * n_lhs + 1]
    acc = res_ref[...]
    for a_ref, w_ref in zip(lhs, ws):
        acc = acc + _dot(a_ref[...], w_ref[...])
    o_ref[...] = acc


def _matmul_res(lhs_list, w_list, res, tm=1024, tn=1024):
    m, n = res.shape
    tm, tn = _tile(m, tm), _tile(n, tn)
    in_specs = ([pl.BlockSpec((tm, a.shape[1]), lambda i, j: (i, 0)) for a in lhs_list]
                + [pl.BlockSpec((w.shape[0], tn), lambda i, j: (0, j)) for w in w_list]
                + [pl.BlockSpec((tm, tn), lambda i, j: (i, j))])
    return pl.pallas_call(
        functools.partial(_matmul_res_kernel, n_lhs=len(lhs_list)),
        grid=(m // tm, n // tn),
        in_specs=in_specs,
        out_specs=pl.BlockSpec((tm, tn), lambda i, j: (i, j)),
        out_shape=jax.ShapeDtypeStruct((m, n), F32),
        compiler_params=_params(("parallel", "parallel"), 56),
        name="matmul_res",
    )(*lhs_list, *w_list, res)


def _ffn_up_kernel(x_ref, halo_ref, g_ref, wa_ref, wb_ref, cwa_ref, cwb_ref, o_ref, xn_ref, sa_ref, sb_ref,
                   *, tm, seq):
    i = pl.program_id(0)

    @pl.when(pl.program_id(1) == 0)
    def _():
        g = g_ref[...]
        starts_sequence = (i * tm) % seq == 0
        halo = jnp.where(starts_sequence, 0.0, _rms(halo_ref[...], g))
        xn_ref[0:HALO, :] = halo.astype(BF16)
        xn_ref[HALO:, :] = _rms(x_ref[...], g).astype(BF16)

    tn = o_ref.shape[1]
    rows, cols = tm // FFN_ROW_PARTS, tn // FFN_COL_PARTS

    def conv(s_ref, cw_ref, rp, cs):
        out = cw_ref[CONV_WIDTH:CONV_WIDTH + 1, cs]
        for tap in range(CONV_WIDTH):
            lo = HALO - (CONV_WIDTH - 1) + tap
            out = out + cw_ref[tap:tap + 1, cs] * s_ref[rp, lo:lo + rows, cs]
        return out

    for rp in range(FFN_ROW_PARTS):
        xn = xn_ref[rp * rows:(rp + 1) * rows + HALO, :]
        for cp in range(FFN_COL_PARTS):
            cs = slice(cp * cols, (cp + 1) * cols)
            sa_ref[rp, :, cs] = _dot(xn, wa_ref[:, cs])
            sb_ref[rp, :, cs] = _dot(xn, wb_ref[:, cs])
            a = conv(sa_ref, cwa_ref, rp, cs)
            b = conv(sb_ref, cwb_ref, rp, cs)
            half = 0.5 * a
            o_ref[rp * rows:(rp + 1) * rows, cs] = (half * (1.0 + jnp.tanh(half)) * b).astype(o_ref.dtype)


def _ffn_up(x, g, w_up, conv_w, conv_b, seq, tm=1024, tn=512):
    m, d = x.shape
    f = w_up.shape[1] // 2
    tm, tn = _tile(seq, tm), _tile(f, tn)
    nj = f // tn
    conv_wb = jnp.concatenate([conv_w, conv_b.reshape(1, 2 * f)], axis=0)
    return pl.pallas_call(
        functools.partial(_ffn_up_kernel, tm=tm, seq=seq),
        grid=(m // tm, nj),
        in_specs=[pl.BlockSpec((tm, d), lambda i, j: (i, 0)),
                  pl.BlockSpec((HALO, d), lambda i, j: (jnp.maximum(i * (tm // HALO) - 1, 0), 0)),
                  pl.BlockSpec((1, d), lambda i, j: (0, 0)),
                  pl.BlockSpec((d, tn), lambda i, j: (0, j)),
                  pl.BlockSpec((d, tn), lambda i, j: (0, j + nj)),
                  pl.BlockSpec((CONV_WIDTH + 1, tn), lambda i, j: (0, j)),
                  pl.BlockSpec((CONV_WIDTH + 1, tn), lambda i, j: (0, j + nj))],
        out_specs=pl.BlockSpec((tm, tn), lambda i, j: (i, j)),
        out_shape=jax.ShapeDtypeStruct((m, f), BF16),
        scratch_shapes=[pltpu.VMEM((tm + HALO, d), BF16),
                        pltpu.VMEM((FFN_ROW_PARTS, tm // FFN_ROW_PARTS + HALO, tn), F32),
                        pltpu.VMEM((FFN_ROW_PARTS, tm // FFN_ROW_PARTS + HALO, tn), F32)],
        compiler_params=_params(("parallel", "arbitrary"), 48),
        name="ffn_up",
    )(x, x, g.reshape(1, d), w_up, w_up, conv_wb, conv_wb)


def _conv_ffn(h, g, w_up, conv_w, conv_b, w_down, seq):
    gated = _ffn_up(h, g, w_up.astype(BF16), conv_w, conv_b, seq)
    return _matmul_res([gated], [w_down.astype(BF16)], h, tn=512)


def _mlstm_kernel(q_ref, k_ref, v_ref, o_ref, grow_ref, gcol_ref, bcol_ref, brow_ref, an_ref, out_ref,
                  c_ref, n_ref, m_ref, *, L):
    H, dh = A_HEADS, A_HEAD_DIM

    @pl.when(pl.program_id(1) == 0)
    def _():
        c_ref[...] = jnp.zeros_like(c_ref)
        n_ref[...] = jnp.zeros_like(n_ref)
        m_ref[...] = jnp.zeros_like(m_ref)

    row = lax.broadcasted_iota(jnp.int32, (L, L), 0)
    col = lax.broadcasted_iota(jnp.int32, (L, L), 1)
    lower = col <= row
    upper = row <= col
    grow = grow_ref[...] + bcol_ref[...]
    gcol = gcol_ref[...] + brow_ref[...]

    for h in range(H):
        hs = slice(h * dh, (h + 1) * dh)
        i_row, lf_row = grow[h:h + 1, :], _log_sigmoid(grow[H + h:H + h + 1, :])
        i_col, lf_col = gcol[:, h:h + 1], _log_sigmoid(gcol[:, H + h:H + h + 1])
        b_col = jnp.sum(jnp.where(lower, lf_row, 0.0), axis=1, keepdims=True)
        b_row = jnp.sum(jnp.where(upper, lf_col, 0.0), axis=0, keepdims=True)
        a_row = i_row - b_row
        a_col = i_col - b_col
        m_prev = m_ref[h][:, 0:1]
        m_col = jnp.maximum(m_prev, jnp.max(jnp.where(lower, a_row, -jnp.inf), axis=1, keepdims=True))
        e = jnp.where(lower, jnp.exp(a_row - m_col), 0.0)

        q = q_ref[:, hs].astype(F32) * (dh ** -0.5)
        kf = k_ref[:, hs].astype(F32)
        qb, kb, vb = q.astype(BF16), kf.astype(BF16), v_ref[:, hs].astype(BF16)
        s = _dot_nt(qb, kb) * e
        w_inter = jnp.exp(m_prev - m_col)
        num = _dot(s.astype(BF16), vb) + w_inter * _dot(qb, c_ref[h].astype(BF16))
        den = jnp.sum(s, axis=1, keepdims=True) + w_inter * jnp.sum(q * n_ref[h], axis=1, keepdims=True)
        hh = num / jnp.maximum(jnp.abs(den), jnp.exp(-(b_col + m_col)))

        m_last = m_col[L - 1:L, :]
        decay = jnp.exp(m_prev - m_last)
        wk = jnp.exp(a_col - m_last) * kf
        c_ref[h] = decay * c_ref[h] + _dot_tn(wk.astype(BF16), vb)
        n_ref[h] = decay * n_ref[h] + jnp.sum(wk, axis=0, keepdims=True)
        m_ref[h] = jnp.broadcast_to(b_col[L - 1:L, :] + m_last, (1, LANES))

        y = _rms(hh, an_ref[h:h + 1, :]) * jax.nn.sigmoid(o_ref[:, hs].astype(F32))
        out_ref[:, hs] = y.astype(out_ref.dtype)


def _mlstm(proj, gates, i_bias, f_bias, a_norm, batch, seq):
    m = proj.shape[0]
    L = _tile(seq, MLSTM_CHUNK)
    nc = seq // L
    H = A_HEADS
    nw = A_WIDTH // A_WIDTH
    del nw
    bias = jnp.concatenate([i_bias, f_bias]).astype(F32)
    bias_col = bias.reshape(2 * H, 1)
    bias_row = jnp.zeros((1, LANES), F32).at[0, :2 * H].set(bias)
    gates_row = gates[:, :2 * H].reshape(batch, seq, 2 * H).transpose(0, 2, 1)
    qkvo = lambda c: pl.BlockSpec((L, A_WIDTH), lambda b, t: (b * nc + t, c))
    return pl.pallas_call(
        functools.partial(_mlstm_kernel, L=L),
        grid=(batch, nc),
        in_specs=[qkvo(0), qkvo(1), qkvo(2), qkvo(3),
                  pl.BlockSpec((None, 2 * H, L), lambda b, t: (b, 0, t)),
                  pl.BlockSpec((L, LANES), lambda b, t: (b * nc + t, 0)),
                  pl.BlockSpec((2 * H, 1), lambda b, t: (0, 0)),
                  pl.BlockSpec((1, LANES), lambda b, t: (0, 0)),
                  pl.BlockSpec((H, A_HEAD_DIM), lambda b, t: (0, 0))],
        out_specs=pl.BlockSpec((L, A_WIDTH), lambda b, t: (b * nc + t, 0)),
        out_shape=jax.ShapeDtypeStruct((m, A_WIDTH), BF16),
        scratch_shapes=[pltpu.VMEM((H, A_HEAD_DIM, A_HEAD_DIM), F32),
                        pltpu.VMEM((H, 1, A_HEAD_DIM), F32),
                        pltpu.VMEM((H, 1, LANES), F32)],
        compiler_params=_params(("parallel", "arbitrary"), 48),
        name="mlstm",
    )(proj, proj, proj, proj, gates_row, gates, bias_col, bias_row, a_norm)


def _sgu_kernel(u_ref, z_ref, sp_ref, sb_ref, bn_ref, o_ref, *, chunks):
    T, gd = B_CHUNK, B_GROUP_DIM
    row = lax.broadcasted_iota(jnp.int32, (T, T), 0)
    col = lax.broadcasted_iota(jnp.int32, (T, T), 1)
    lower = col <= row
    for g in range(B_GROUPS):
        gs = slice(g * gd, (g + 1) * gd)
        w = jnp.where(lower, sp_ref[g], 0.0).astype(BF16)
        bias = sb_ref[:, g:g + 1]
        bn = bn_ref[g:g + 1, :]
        for c in range(chunks):
            rs = slice(c * T, (c + 1) * T)
            zn = _rms(jax.nn.gelu(z_ref[rs, gs].astype(F32)), bn)
            zmix = _dot(w, zn.astype(BF16)) + bias
            o_ref[rs, gs] = (jax.nn.gelu(u_ref[rs, gs].astype(F32)) * zmix).astype(o_ref.dtype)


def _sgu(proj, spatial, spatial_bias, b_norm, seq, u_block, z_block):
    m = proj.shape[0]
    tm = _tile(seq, 512)
    chunks = tm // B_CHUNK
    return pl.pallas_call(
        functools.partial(_sgu_kernel, chunks=chunks),
        grid=(m // tm,),
        in_specs=[pl.BlockSpec((tm, B_WIDTH), lambda i: (i, u_block)),
                  pl.BlockSpec((tm, B_WIDTH), lambda i: (i, z_block)),
                  pl.BlockSpec((B_GROUPS, B_CHUNK, B_CHUNK), lambda i: (0, 0, 0)),
                  pl.BlockSpec((B_CHUNK, B_GROUPS), lambda i: (0, 0)),
                  pl.BlockSpec((B_GROUPS, B_GROUP_DIM), lambda i: (0, 0))],
        out_specs=pl.BlockSpec((tm, B_WIDTH), lambda i: (i, 0)),
        out_shape=jax.ShapeDtypeStruct((m, B_WIDTH), BF16),
        compiler_params=_params(("parallel",), 48),
        name="sgu",
    )(proj, proj, spatial, spatial_bias.T, b_norm)


def _even_mixer(h, g, w_in, i_bias, f_bias, a_norm, b_norm, spatial, spatial_bias, w_out, batch, seq):
    d = h.shape[1]
    n_main = 4 * A_WIDTH
    w_main = jnp.concatenate([w_in[:, :n_main], w_in[:, n_main + 2 * A_HEADS:]], axis=1).astype(BF16)
    w_gate = jnp.zeros((d, LANES), F32).at[:, :2 * A_HEADS].set(w_in[:, n_main:n_main + 2 * A_HEADS]).astype(BF16)
    proj = _norm_matmul(h, g, w_main, out_dtype=BF16)
    gates = _norm_matmul(h, g, w_gate, tn=LANES)
    h_a = _mlstm(proj, gates, i_bias, f_bias, a_norm, batch, seq)
    blk = n_main // B_WIDTH
    h_b = _sgu(proj, spatial, spatial_bias, b_norm, seq, blk, blk + 1)
    w_out = w_out.astype(BF16)
    return _matmul_res([h_a, h_b], [w_out[:A_WIDTH], w_out[A_WIDTH:]], h)


def _rope_tables(seq):
    pos = jnp.arange(seq, dtype=jnp.int32).astype(F32)

    def angles(d):
        inv = jnp.power(jnp.float32(ROPE_THETA), -jnp.arange(0, d, 2, dtype=F32) / d)
        ang = pos[:, None] * inv[None, :]
        return jnp.cos(ang), jnp.sin(ang)

    cos, sin = angles(C_HEAD_DIM)
    cq = jnp.concatenate([cos, cos], axis=1)
    sq = jnp.concatenate([-sin, sin], axis=1)
    cos, sin = angles(IDX_ROPE_DIM)
    half = IDX_ROPE_DIM // 2
    rest = IDX_DIM - IDX_ROPE_DIM
    zeros_h, zeros_r = jnp.zeros((seq, half), F32), jnp.zeros((seq, rest), F32)
    ci = jnp.concatenate([cos, cos, jnp.ones((seq, rest), F32)], axis=1)
    s_from_left = jnp.concatenate([zeros_h, sin, zeros_r], axis=1)
    s_from_right = jnp.concatenate([-sin, zeros_h, zeros_r], axis=1)
    rep = LANES // IDX_DIM
    return cq, sq, jnp.tile(ci, (1, rep)), jnp.tile(s_from_left, (1, rep)), jnp.tile(s_from_right, (1, rep))


def _odd_prep_kernel(main_ref, misc_ref, cq_ref, sq_ref, ci_ref, sl_ref, sr_ref, qn_ref, kn_ref, lg_ref, lb_ref,
                     q_out, k_out, v_out, iq_out, ik_out, iw_out):
    dh = C_HEAD_DIM
    cq, sq = cq_ref[...], sq_ref[...]
    ci, sl, sr = ci_ref[...], sl_ref[...], sr_ref[...]
    half = IDX_ROPE_DIM // 2

    def rope_full(x):
        return x * cq + pltpu.roll(x, dh // 2, 1) * sq

    def rope_idx(x):
        return x * ci + pltpu.roll(x, half, 1) * sl + pltpu.roll(x, LANES - half, 1) * sr

    qn = qn_ref[...] * (dh ** -0.5 * LOG2_E)
    for h in range(C_HEADS):
        hs = slice(h * dh, (h + 1) * dh)
        q_out[h] = rope_full(_rms(main_ref[:, hs], qn)).astype(q_out.dtype)
    k0 = C_HEADS * dh
    for h in range(C_KV_HEADS):
        hs = slice(h * dh, (h + 1) * dh)
        src = slice(k0 + h * dh, k0 + (h + 1) * dh)
        k_out[:, hs] = rope_full(_rms(main_ref[:, src], kn_ref[...])).astype(k_out.dtype)
    v0 = k0 + C_KV_HEADS * dh
    v_out[...] = main_ref[:, v0:v0 + C_KV_HEADS * dh].astype(v_out.dtype)
    i0 = v0 + C_KV_HEADS * dh
    per_tile = LANES // IDX_DIM
    for t in range(IDX_HEADS // per_tile):
        y = rope_idx(main_ref[:, i0 + t * LANES:i0 + (t + 1) * LANES]).T
        for j in range(per_tile):
            iq_out[t * per_tile + j] = y[j * IDX_DIM:(j + 1) * IDX_DIM, :].astype(iq_out.dtype)

    misc = misc_ref[...]
    lane = lax.broadcasted_iota(jnp.int32, misc.shape, 1)
    is_key = lane < IDX_DIM
    mu = jnp.sum(jnp.where(is_key, misc, 0.0), axis=1, keepdims=True) / IDX_DIM
    cen = jnp.where(is_key, misc - mu, 0.0)
    var = jnp.sum(cen * cen, axis=1, keepdims=True) / IDX_DIM
    ikn = cen * lax.rsqrt(var + NORM_EPS) * lg_ref[...] + lb_ref[...]
    ik_out[...] = rope_idx(ikn)[:, :IDX_DIM].astype(ik_out.dtype)
    iw_out[...] = misc.T[IDX_DIM:IDX_DIM + IDX_HEADS, :] * (IDX_HEADS ** -0.5 * IDX_DIM ** -0.5)


def _odd_prep(main, misc, q_norm, k_norm, ln_g, ln_b, seq):
    m = main.shape[0]
    tm = _tile(seq, 256)
    ns = seq // tm
    tables = _rope_tables(seq)
    pad = lambda v: jnp.zeros((1, LANES), F32).at[0, :IDX_DIM].set(v)
    tab = pl.BlockSpec((tm, LANES), lambda i: (i % ns, 0))
    vec = pl.BlockSpec((1, LANES), lambda i: (0, 0))
    qw, kw = C_HEADS * C_HEAD_DIM, C_KV_HEADS * C_HEAD_DIM
    return pl.pallas_call(
        _odd_prep_kernel,
        grid=(m // tm,),
        in_specs=[pl.BlockSpec((tm, main.shape[1]), lambda i: (i, 0)),
                  pl.BlockSpec((tm, LANES), lambda i: (i, 0)),
                  tab, tab, tab, tab, tab, vec, vec, vec, vec],
        out_specs=[pl.BlockSpec((C_HEADS, tm, C_HEAD_DIM), lambda i: (0, i, 0)),
                   pl.BlockSpec((tm, kw), lambda i: (i, 0)),
                   pl.BlockSpec((tm, kw), lambda i: (i, 0)),
                   pl.BlockSpec((IDX_HEADS, IDX_DIM, tm), lambda i: (0, 0, i)),
                   pl.BlockSpec((tm, IDX_DIM), lambda i: (i, 0)),
                   pl.BlockSpec((IDX_HEADS, tm), lambda i: (0, i))],
        out_shape=[jax.ShapeDtypeStruct((C_HEADS, m, C_HEAD_DIM), BF16),
                   jax.ShapeDtypeStruct((m, kw), BF16),
                   jax.ShapeDtypeStruct((m, kw), BF16),
                   jax.ShapeDtypeStruct((IDX_HEADS, IDX_DIM, m), BF16),
                   jax.ShapeDtypeStruct((m, IDX_DIM), BF16),
                   jax.ShapeDtypeStruct((IDX_HEADS, m), F32)],
        compiler_params=_params(("parallel",), 48),
        name="odd_prep",
    )(main, misc, *tables, q_norm.reshape(1, LANES), k_norm.reshape(1, LANES), pad(ln_g), pad(ln_b))


def _indexer_kernel(iq_ref, iw_ref, ik_ref, o_ref, key_ref, hi_ref, lo_ref, *, tq, tk, tko, nkb, topk, pos_bits):
    qi = pl.program_id(1)
    n_causal = ((qi + 1) * tq + tk - 1) // tk
    q_pos = qi * tq + lax.broadcasted_iota(jnp.int32, (1, tq), 1)
    sub = lax.broadcasted_iota(jnp.int32, (SUBLANES, tq), 0)
    iw = iw_ref[...]

    def score_body(kb, carry):
        k0 = pl.multiple_of(kb * tk, tk)
        for c in range(tk // IDX_KEYS):
            ikc = ik_ref[pl.ds(k0 + c * IDX_KEYS, IDX_KEYS), :]
            acc = jnp.zeros((IDX_KEYS, tq), F32)
            for h in range(IDX_HEADS):
                acc = acc + iw[h:h + 1, :] * jnp.maximum(_dot(ikc, iq_ref[h]), 0.0)
            bits = lax.bitcast_convert_type(acc, jnp.int32)
            key = bits ^ ((bits >> 31) & jnp.int32(0x7FFFFFFF))
            k_pos = k0 + c * IDX_KEYS + lax.broadcasted_iota(jnp.int32, (IDX_KEYS, 1), 0)
            key = jnp.where(k_pos <= q_pos, key, INT_MIN)
            cs = slice(c * IDX_KEYS, (c + 1) * IDX_KEYS)
            key_ref[kb, cs, :] = key
            hi_ref[kb, cs, :] = (key >> 16).astype(jnp.int16)
            lo_ref[kb, cs, :] = ((key & 0xFFFF) - HALF_RANGE).astype(jnp.int16)
        return carry

    lax.fori_loop(0, n_causal, score_body, 0)

    def count16(ref, pred):
        def body(kb, cs):
            cs = list(cs)
            for g in range(tk // PACKED_ROWS):
                hit = pred(ref[kb, g * PACKED_ROWS:(g + 1) * PACKED_ROWS, :])
                cs[g % len(cs)] = cs[g % len(cs)] + jnp.where(hit, jnp.int16(1), jnp.int16(0))
            return tuple(cs)
        zero = jnp.zeros((PACKED_ROWS, tq), jnp.int16)
        cs = lax.fori_loop(0, n_causal, body, (zero,) * IDX_ACCS)
        total = functools.reduce(jnp.add, cs).astype(jnp.int32)
        return jnp.broadcast_to(jnp.sum(total, axis=0, keepdims=True), (SUBLANES, tq))

    def packed_row(v):
        return jnp.broadcast_to(v[0:1, :], (PACKED_ROWS, tq)).astype(jnp.int16)

    def kth_largest16(ref, k):
        def step(it, thr):
            cand = thr + jnp.left_shift(jnp.int32(1), 15 - it)
            cand16 = packed_row(cand)
            return jnp.where(count16(ref, lambda v: v >= cand16) >= k, cand, thr)
        return lax.fori_loop(0, 16, step, jnp.full((SUBLANES, tq), -HALF_RANGE, jnp.int32))

    def count(pred):
        def body(kb, cs):
            cs = list(cs)
            for g in range(tk // SUBLANES):
                key = key_ref[kb, g * SUBLANES:(g + 1) * SUBLANES, :]
                hit = pred(key, kb * tk + g * SUBLANES + sub).astype(jnp.int32)
                cs[g % len(cs)] = cs[g % len(cs)] + hit
            return tuple(cs)
        zero = jnp.zeros((SUBLANES, tq), jnp.int32)
        cs = lax.fori_loop(0, n_causal, body, (zero,) * IDX_ACCS)
        total = functools.reduce(jnp.add, cs)
        return jnp.broadcast_to(jnp.sum(total, axis=0, keepdims=True), (SUBLANES, tq))

    thr_hi = kth_largest16(hi_ref, topk)
    thr_hi16 = packed_row(thr_hi)
    rank_lo = topk - count16(hi_ref, lambda v: v > thr_hi16)

    def mask_low(kb, carry):
        for g in range(tk // PACKED_ROWS):
            gs = slice(g * PACKED_ROWS, (g + 1) * PACKED_ROWS)
            lo_ref[kb, gs, :] = jnp.where(hi_ref[kb, gs, :] == thr_hi16, lo_ref[kb, gs, :], jnp.int16(-HALF_RANGE))
        return carry

    lax.fori_loop(0, n_causal, mask_low, 0)
    thr_lo = kth_largest16(lo_ref, rank_lo)
    thr = thr_hi * (2 * HALF_RANGE) + (thr_lo + HALF_RANGE)

    def tie_limit():
        keep = topk - count(lambda key, pos: key > thr)

        def step(it, below):
            cand = below + jnp.left_shift(jnp.int32(1), pos_bits - 1 - it)
            n_le = count(lambda key, pos: (key == thr) & (pos <= cand))
            return jnp.where(n_le < keep, cand, below)

        below = lax.fori_loop(0, pos_bits, step, jnp.full((SUBLANES, tq), -1, jnp.int32))
        return below + 1

    n_ge = count(lambda key, pos: key >= thr)
    has_excess_ties = jnp.max(n_ge.astype(F32)) > topk
    lim = lax.cond(has_excess_ties, tie_limit, lambda: jnp.full((SUBLANES, tq), 2 ** pos_bits, jnp.int32))
    thr_row, lim_row = thr[0:1, :], lim[0:1, :]

    def write_body(kb, carry):
        for j in range(tk // LANES):
            key = key_ref[kb, j * LANES:(j + 1) * LANES, :]
            pos = kb * tk + j * LANES + lax.broadcasted_iota(jnp.int32, (LANES, 1), 0)
            sel = ((key > thr_row) | ((key == thr_row) & (pos <= lim_row))) & (pos <= q_pos)
            bias = jnp.where(sel, 0.0, MASK_NEG)
            off = j * LANES
            o_ref[kb * (tk // tko) + off // tko, :, off % tko:off % tko + LANES] = bias.T.astype(o_ref.dtype)
        return carry

    lax.fori_loop(0, n_causal, write_body, 0)

    def fill_body(ob, carry):
        o_ref[ob] = jnp.full((tq, tko), MASK_NEG, o_ref.dtype)
        return carry

    lax.fori_loop(n_causal * (tk // tko), nkb * (tk // tko), fill_body, 0)


def _indexer(iq_t, iw_t, ik, batch, seq, tq, tk, tko):
    nkb = seq // tk
    nq = seq // tq
    topk = min(TOPK_MAX, seq // 4)
    assert tk % tko == 0 and tko % LANES == 0
    return pl.pallas_call(
        functools.partial(_indexer_kernel, tq=tq, tk=tk, tko=tko, nkb=nkb, topk=topk,
                          pos_bits=max(1, (seq - 1).bit_length())),
        grid=(batch, nq),
        in_specs=[pl.BlockSpec((IDX_HEADS, IDX_DIM, tq), lambda b, i: (0, 0, b * nq + i)),
                  pl.BlockSpec((IDX_HEADS, tq), lambda b, i: (0, b * nq + i)),
                  pl.BlockSpec((seq, IDX_DIM), lambda b, i: (b, 0))],
        out_specs=pl.BlockSpec((None, seq // tko, tq, tko), lambda b, i: (b, 0, i, 0)),
        out_shape=jax.ShapeDtypeStruct((batch, seq // tko, seq, tko), BF16),
        scratch_shapes=[pltpu.VMEM((nkb, tk, tq), jnp.int32),
                        pltpu.VMEM((nkb, tk, tq), jnp.int16),
                        pltpu.VMEM((nkb, tk, tq), jnp.int16)],
        compiler_params=_params(("parallel", "parallel"), 56),
        name="indexer",
    )(iq_t, iw_t, ik)


def _attn_kernel(qi_ref, kb_ref, q_ref, k_ref, v_ref, b_ref, o_ref, acc_ref, m_ref, l_ref, s_ref, p_ref, bias_ref,
                 *, tq, tk):
    dh = C_HEAD_DIM
    group = C_HEADS // C_KV_HEADS
    pair = pl.program_id(1)
    qi, kb = qi_ref[pair], kb_ref[pair]
    last_kb = (qi * tq + tq - 1) // tk

    @pl.when(kb == 0)
    def _():
        m_ref[...] = jnp.full_like(m_ref, MASK_NEG)
        l_ref[...] = jnp.zeros_like(l_ref)
        acc_ref[...] = jnp.zeros_like(acc_ref)

    rows = group * tq
    bias_ref[...] = b_ref[...].astype(F32)

    def logits(g):
        qg = q_ref[g * group:(g + 1) * group].reshape(rows, dh)
        s_ref[g % 2] = _dot_nt(qg, k_ref[:, g * dh:(g + 1) * dh])

    logits(0)
    for g in range(C_KV_HEADS):
        if g + 1 < C_KV_HEADS:
            logits(g + 1)
        for c in range(rows // ATTN_ROWS):
            rs = slice(c * ATTN_ROWS, (c + 1) * ATTN_ROWS)
            brs = slice((c * ATTN_ROWS) % tq, (c * ATTN_ROWS) % tq + ATTN_ROWS)
            tiles = [s_ref[g % 2, rs, t * LANES:(t + 1) * LANES] + bias_ref[brs, t * LANES:(t + 1) * LANES]
                     for t in range(tk // LANES)]
            tile_max = functools.reduce(jnp.maximum, tiles)
            m_old = m_ref[g, rs, :]
            m_new = jnp.maximum(m_old, jnp.max(tile_max, axis=1, keepdims=True))
            alpha = jnp.exp2(m_old - m_new)
            ps = [jnp.exp2(t - m_new) for t in tiles]
            l_ref[g, rs, :] = alpha * l_ref[g, rs, :] + jnp.sum(functools.reduce(jnp.add, ps), axis=1,
                                                              keepdims=True)
            m_ref[g, rs, :] = m_new
            acc_ref[g, rs, :] = alpha * acc_ref[g, rs, :]
            for t, p in enumerate(ps):
                p_ref[g % 2, rs, t * LANES:(t + 1) * LANES] = p.astype(BF16)
        acc_ref[g] += _dot(p_ref[g % 2], v_ref[:, g * dh:(g + 1) * dh])

    @pl.when(kb == last_kb)
    def _():
        for g in range(C_KV_HEADS):
            out = acc_ref[g] / l_ref[g]
            for j in range(group):
                h = g * group + j
                o_ref[:, h * dh:(h + 1) * dh] = out[j * tq:(j + 1) * tq].astype(o_ref.dtype)


def _attention(q, k, v, mask, batch, seq, tq, tk):
    m = k.shape[0]
    nq, nkb = seq // tq, seq // tk
    qw, kw = C_HEADS * C_HEAD_DIM, C_KV_HEADS * C_HEAD_DIM
    rows = (C_HEADS // C_KV_HEADS) * tq
    pairs = [(i, j) for i in range(nq) for j in range((i * tq + tq - 1) // tk + 1)]
    qi_of = jnp.asarray([p[0] for p in pairs], jnp.int32)
    kb_of = jnp.asarray([p[1] for p in pairs], jnp.int32)
    grid_spec = pltpu.PrefetchScalarGridSpec(
        num_scalar_prefetch=2,
        grid=(batch, len(pairs)),
        in_specs=[pl.BlockSpec((C_HEADS, tq, C_HEAD_DIM), lambda b, p, qi, kb: (0, b * nq + qi[p], 0)),
                  pl.BlockSpec((tk, kw), lambda b, p, qi, kb: (b * nkb + kb[p], 0)),
                  pl.BlockSpec((tk, kw), lambda b, p, qi, kb: (b * nkb + kb[p], 0)),
                  pl.BlockSpec((None, None, tq, tk), lambda b, p, qi, kb: (b, kb[p], qi[p], 0))],
        out_specs=pl.BlockSpec((tq, qw), lambda b, p, qi, kb: (b * nq + qi[p], 0)),
        scratch_shapes=[pltpu.VMEM((C_KV_HEADS, rows, C_HEAD_DIM), F32),
                        pltpu.VMEM((C_KV_HEADS, rows, LANES), F32),
                        pltpu.VMEM((C_KV_HEADS, rows, LANES), F32),
                        pltpu.VMEM((2, rows, tk), F32),
                        pltpu.VMEM((2, rows, tk), BF16),
                        pltpu.VMEM((tq, tk), F32)])
    return pl.pallas_call(
        functools.partial(_attn_kernel, tq=tq, tk=tk),
        grid_spec=grid_spec,
        out_shape=jax.ShapeDtypeStruct((m, qw), BF16),
        compiler_params=_params(("parallel", "arbitrary"), 48),
        name="attention",
    )(qi_of, kb_of, q, k, v, mask)


def _odd_mixer(h, g, w_in, q_norm, k_norm, ln_g, ln_b, w_out, batch, seq):
    d = h.shape[1]
    n_main = (C_HEADS + 2 * C_KV_HEADS) * C_HEAD_DIM + IDX_HEADS * IDX_DIM
    n_misc = IDX_DIM + IDX_HEADS
    w_main = w_in[:, :n_main].astype(BF16)
    w_misc = jnp.zeros((d, LANES), F32).at[:, :n_misc].set(w_in[:, n_main:]).astype(BF16)
    main = _norm_matmul(h, g, w_main)
    misc = _norm_matmul(h, g, w_misc, tn=LANES)
    q, k, v, iq, ik, iw = _odd_prep(main, misc, q_norm, k_norm, ln_g, ln_b, seq)
    tq_idx, tk_idx = _tile(seq, 256), _tile(seq, 1024)
    tq_att, tk_att = _tile(seq, 256), _tile(seq, 512)
    mask = _indexer(iq, iw, ik, batch, seq, tq_idx, tk_idx, tk_att)
    att = _attention(q, k, v, mask, batch, seq, tq_att, tk_att)
    return _matmul_res([att], [w_out.astype(BF16)], h)


def kernel(x, mix_norm, even_w_in, even_i_bias, even_f_bias, even_a_norm, even_b_norm, even_spatial,
           even_spatial_bias, even_w_out, odd_w_in, odd_q_norm, odd_k_norm, odd_idx_ln_g, odd_idx_ln_b,
           odd_w_out, ffn_norm, ffn_w_up, ffn_conv_w, ffn_conv_b, ffn_w_down):
    batch, seq, d = x.shape
    depth = mix_norm.shape[0]
    h = x.reshape(batch * seq, d)
    for layer in range(depth):
        j = layer // 2
        if layer % 2 == 0:
            h = _even_mixer(h, mix_norm[layer], even_w_in[j], even_i_bias[j], even_f_bias[j], even_a_norm[j],
                            even_b_norm[j], even_spatial[j], even_spatial_bias[j], even_w_out[j], batch, seq)
        else:
            h = _odd_mixer(h, mix_norm[layer], odd_w_in[j], odd_q_norm[j], odd_k_norm[j], odd_idx_ln_g[j],
                           odd_idx_ln_b[j], odd_w_out[j], batch, seq)
        h = _conv_ffn(h, ffn_norm[layer], ffn_w_up[layer], ffn_conv_w[layer], ffn_conv_b[layer],
                      ffn_w_down[layer], seq)
    return h.reshape(batch, seq, d)
```
